```python
import jax, jax.numpy as jnp
from jax import lax
import numpy as np

D_MODEL = 1024
BATCH = 4
SEQ = 4096
DEPTH = 2

GRID_W = 64
MEM_LEN = 256
HG_HEADS = 8
HG_DIM = 128
HG_WIDTH = HG_HEADS * HG_DIM
HG_CHUNK = 64
NA_HEADS = 8
NA_DIM = 64
NA_WIDTH = NA_HEADS * NA_DIM
NA_KH = 8
NA_KW = 16
CA_HEADS = 4
CA_DIM = 128
CA_WIDTH = CA_HEADS * CA_DIM
N_BRANCH = 3
D_FF = 2816
CONV_W = 3
EPS = 1e-6
F_FLOOR = 1e-12
MASK_NEG = -1e30
IN_SIZES = (HG_WIDTH,) * 5 + (NA_WIDTH,) * 3 + (CA_WIDTH,) + (D_MODEL,) * N_BRANCH
IN_COLS = sum(IN_SIZES)

kernel_name = 'hybrid_hgrn2_natten_memxattn_convffn'


def rmsnorm(x, g):
    xf = x.astype(jnp.float32)
    y = xf * lax.rsqrt(jnp.mean(xf * xf, axis=-1, keepdims=True) + EPS)
    return (y * g.astype(jnp.float32)).astype(x.dtype)


def to_heads(a, n):
    b, t, w = a.shape
    return a.reshape(b, t, n, w // n).transpose(0, 2, 1, 3)


def split_cols(a):
    return jnp.split(a, np.cumsum(IN_SIZES)[:-1].tolist(), axis=-1)


def gla_chunk_scan(q, k, v, log_f):
    b, h, t, dk = q.shape
    dv = v.shape[-1]
    n = t // HG_CHUNK

    def chunks(a):
        return jnp.moveaxis(a.reshape(b, h, n, HG_CHUNK, a.shape[-1]), 2, 0)

    tri = jnp.tril(jnp.ones((HG_CHUNK, HG_CHUNK), dtype=bool))[:, :, None]

    def step(S, inp):
        qc, kc, vc, gc = inp
        cum = jnp.cumsum(gc, axis=2)
        diff = cum[:, :, :, None, :] - cum[:, :, None, :, :]
        decay = jnp.where(tri, jnp.exp(jnp.where(tri, diff, 0.0)), 0.0)
        scores = jnp.einsum('bhtk,bhsk,bhtsk->bhts', qc, kc, decay)
        o = jnp.einsum('bhts,bhsv->bhtv', scores, vc) + jnp.einsum('bhtk,bhkv->bhtv', qc * jnp.exp(cum), S)
        last = cum[:, :, -1:, :]
        S = jnp.exp(last[:, :, 0, :, None]) * S + jnp.einsum('bhsk,bhsv->bhkv', kc * jnp.exp(last - cum), vc)
        return S, o

    S0 = jnp.zeros((b, h, dk, dv), q.dtype)
    _, o = lax.scan(step, S0, (chunks(q), chunks(k), chunks(v), chunks(log_f)))
    return jnp.moveaxis(o, 0, 2).reshape(b, h, t, dv)


def hgrn2_branch(zq, zi, zf_fwd, zf_bwd, zg, lb, gnorm, w_o):
    f32 = jnp.float32
    dt = zq.dtype
    b, t = zq.shape[:2]
    q = to_heads(jax.nn.silu(zq.astype(f32)) * HG_DIM ** -0.5, HG_HEADS)
    v = to_heads(zi.astype(f32), HG_HEADS)

    def one_direction(zf, lb_d, reverse):
        zf = zf.astype(f32)
        f = lb_d + (1.0 - lb_d) * jax.nn.sigmoid(zf)
        log_f = jnp.log(jnp.maximum(f, F_FLOOR))
        k = (1.0 - lb_d) * jax.nn.sigmoid(-zf)
        ins = (q, to_heads(k, HG_HEADS), v, to_heads(log_f, HG_HEADS))
        if reverse:
            ins = tuple(jnp.flip(a, axis=2) for a in ins)
            return jnp.flip(gla_chunk_scan(*ins), axis=2)
        return gla_chunk_scan(*ins)

    o = one_direction(zf_fwd, lb[0], False) + one_direction(zf_bwd, lb[1], True)
    o = rmsnorm(o.transpose(0, 2, 1, 3), gnorm)
    gate = jax.nn.silu(zg.astype(f32)).reshape(o.shape)
    o = (o * gate).reshape(b, t, HG_WIDTH).astype(dt)
    return o @ w_o


def neighbourhood_attention(zq, zk, zv, rpb, w_o):
    b, t, _ = zq.shape
    rows = t // GRID_W
    kh = min(NA_KH, rows)
    r = np.arange(rows)
    row_idx = np.clip(r - kh // 2, 0, rows - kh)[:, None] + np.arange(kh)[None, :]
    c = np.arange(GRID_W)
    col_start = np.clip(c - NA_KW // 2, 0, GRID_W - NA_KW)
    col_mask = (c[None, :] >= col_start[:, None]) & (c[None, :] < col_start[:, None] + NA_KW)
    dr = row_idx - r[:, None] + (NA_KH - 1)
    dc = np.clip(c[None, :] - c[:, None], -(NA_KW - 1), NA_KW - 1) + (NA_KW - 1)
    bias = rpb[:, dr[:, None, :, None], dc[None, :, None, :]]

    def grid(a):
        return a.reshape(b, rows, GRID_W, NA_HEADS, NA_DIM)

    q = grid(zq) * NA_DIM ** -0.5
    k = grid(zk)[:, row_idx]
    v = grid(zv)[:, row_idx]
    s = jnp.einsum('brqhd,brkwhd->bhrqkw', q, k).astype(jnp.float32) + bias.astype(jnp.float32)
    s = jnp.where(col_mask[:, None, :], s, MASK_NEG)
    p = jax.nn.softmax(s.reshape(b, NA_HEADS, rows, GRID_W, kh * GRID_W), axis=-1)
    p = p.reshape(s.shape).astype(zv.dtype)
    o = jnp.einsum('bhrqkw,brkwhd->brqhd', p, v).reshape(b, t, NA_WIDTH)
    return o @ w_o


def memory_cross_attention(zq, mem_n, w_kv, w_o):
    b, t, _ = zq.shape
    m = mem_n.shape[1]
    q = zq.reshape(b, t, CA_HEADS, CA_DIM) * CA_DIM ** -0.5
    kv = (mem_n @ w_kv).reshape(b, m, 2, CA_HEADS, CA_DIM)
    s = jnp.einsum('bthd,bmhd->bhtm', q, kv[:, :, 0]).astype(jnp.float32)
    p = jax.nn.softmax(s, axis=-1).astype(zq.dtype)
    o = jnp.einsum('bhtm,bmhd->bthd', p, kv[:, :, 1]).reshape(b, t, CA_WIDTH)
    return o @ w_o


def conv_ffn(h, w_up, conv_w, conv_b, w_down):
    t = h.shape[1]
    u = h @ w_up
    up = jnp.pad(u, ((0, 0), (CONV_W // 2, CONV_W // 2), (0, 0)))
    u = sum(up[:, j:j + t] * conv_w[j] for j in range(CONV_W)) + conv_b
    a, g = jnp.split(u, 2, axis=-1)
    return (jax.nn.gelu(a) * g) @ w_down


def setup_inputs(seed: int = 0) -> dict:
    key = jax.random.key(seed)
    ks = jax.random.split(key, 19)

    def nrm(k, shape, scale):
        return jax.random.normal(k, shape, jnp.float32) * scale

    return {
        'x': nrm(ks[0], (BATCH, SEQ, D_MODEL), 1.0),
        'mem': nrm(ks[1], (BATCH, MEM_LEN, D_MODEL), 1.0),
        'norm_mix': 1.0 + nrm(ks[2], (DEPTH, D_MODEL), 0.02),
        'w_in': nrm(ks[3], (DEPTH, D_MODEL, IN_COLS), D_MODEL ** -0.5),
        'hg_lb_logits': nrm(ks[4], (DEPTH, 2, HG_WIDTH), 1.0),
        'hg_gnorm': 1.0 + nrm(ks[5], (DEPTH, HG_DIM), 0.02),
        'w_hg_o': nrm(ks[6], (DEPTH, HG_WIDTH, D_MODEL), HG_WIDTH ** -0.5),
        'na_rpb': nrm(ks[7], (DEPTH, NA_HEADS, 2 * NA_KH - 1, 2 * NA_KW - 1), 0.1),
        'w_na_o': nrm(ks[8], (DEPTH, NA_WIDTH, D_MODEL), NA_WIDTH ** -0.5),
        'mem_norm': 1.0 + nrm(ks[9], (D_MODEL,), 0.02),
        'w_mem_kv': nrm(ks[10], (DEPTH, D_MODEL, 2 * CA_WIDTH), D_MODEL ** -0.5),
        'w_ca_o': nrm(ks[11], (DEPTH, CA_WIDTH, D_MODEL), CA_WIDTH ** -0.5),
        'w_out': nrm(ks[12], (DEPTH, D_MODEL, D_MODEL), D_MODEL ** -0.5),
        'norm_ffn': 1.0 + nrm(ks[13], (DEPTH, D_MODEL), 0.02),
        'w_up': nrm(ks[14], (DEPTH, D_MODEL, 2 * D_FF), D_MODEL ** -0.5),
        'conv_w': nrm(ks[15], (DEPTH, CONV_W, 2 * D_FF), CONV_W ** -0.5),
        'conv_b': nrm(ks[16], (DEPTH, 2 * D_FF), 0.01),
        'w_down': nrm(ks[17], (DEPTH, D_FF, D_MODEL), D_FF ** -0.5),
        'norm_final': 1.0 + nrm(ks[18], (D_MODEL,), 0.02),
    }


def reference(x, mem, norm_mix, w_in, hg_lb_logits, hg_gnorm, w_hg_o, na_rpb, w_na_o, mem_norm,
              w_mem_kv, w_ca_o, w_out, norm_ffn, w_up, conv_w, conv_b, w_down, norm_final):
    p_lb = jax.nn.softmax(hg_lb_logits.astype(jnp.float32), axis=0)
    lower_bounds = jnp.clip(jnp.cumsum(p_lb, axis=0) - p_lb[0], 0.0, 1.0)
    mem_n = rmsnorm(mem, mem_norm)
    for l in range(DEPTH):
        h = rmsnorm(x, norm_mix[l])
        (hq, hi, hff, hfb, hg, nq, nk, nv, cq, g_hg, g_na, g_ca) = split_cols(h @ w_in[l])
        y_hg = hgrn2_branch(hq, hi, hff, hfb, hg, lower_bounds[l], hg_gnorm[l], w_hg_o[l])
        y_na = neighbourhood_attention(nq, nk, nv, na_rpb[l], w_na_o[l])
        y_ca = memory_cross_attention(cq, mem_n, w_mem_kv[l], w_ca_o[l])
        merged = jax.nn.sigmoid(g_hg) * y_hg + jax.nn.sigmoid(g_na) * y_na + jax.nn.sigmoid(g_ca) * y_ca
        x = x + merged @ w_out[l]
        x = x + conv_ffn(rmsnorm(x, norm_ffn[l]), w_up[l], conv_w[l], conv_b[l], w_down[l])
    return rmsnorm(x, norm_final)
```

```python
import functools

import numpy as np
import jax
import jax.numpy as jnp
from jax import lax
from jax.experimental import pallas as pl
from jax.experimental.pallas import tpu as pltpu

D_MODEL = 1024
GRID_W = 64
HG_HEADS = 8
HG_DIM = 128
NA_HEADS = 8
NA_DIM = 64
NA_KH = 8
NA_KW = 16
CA_HEADS = 4
CA_DIM = 128
D_FF = 2816
CONV_W = 3
EPS = 1e-6
F_FLOOR = 1e-12
MASK_NEG = -1e30

LANES = 128
SUBLANES = 8
VMEM_LIMIT = 48 * 1024 * 1024

HG_CHUNK = 128

CB_HQ, CB_HI, CB_HG = 0, 8, 16
CB_NQ, CB_NK, CB_NV = 24, 28, 32
CB_CQ = 36
CB_GHG, CB_GNA, CB_GCA = 40, 48, 56
N_CB = 64

F32 = jnp.float32
BF16 = jnp.bfloat16


def _cparams(*sem):
    return pltpu.CompilerParams(dimension_semantics=sem, vmem_limit_bytes=VMEM_LIMIT)


def _sigmoid(x):
    return 1.0 / (1.0 + jnp.exp(-x))


def _norm_mm_kernel(x_ref, g_ref, w_ref, o_ref, h_ref):
    @pl.when(pl.program_id(1) == 0)
    def _():
        x = x_ref[...]
        ms = jnp.mean(x * x, axis=-1, keepdims=True)
        h_ref[...] = (x * lax.rsqrt(ms + EPS) * g_ref[...]).astype(BF16)

    acc = jnp.dot(h_ref[...], w_ref[...], preferred_element_type=F32)
    for j in range(o_ref.shape[0]):
        o_ref[j] = acc[:, j * LANES:(j + 1) * LANES].astype(o_ref.dtype)


def _norm_matmul(x2, gain, w_bf16, out_dtype, tm, tn):
    m, d = x2.shape
    n = w_bf16.shape[1]
    return pl.pallas_call(
        _norm_mm_kernel,
        grid=(m // tm, n // tn),
        in_specs=[
            pl.BlockSpec((tm, d), lambda i, j: (i, 0)),
            pl.BlockSpec((1, d), lambda i, j: (0, 0)),
            pl.BlockSpec((d, tn), lambda i, j: (0, j)),
        ],
        out_specs=pl.BlockSpec((tn // LANES, tm, LANES), lambda i, j: (j, i, 0)),
        out_shape=jax.ShapeDtypeStruct((n // LANES, m, LANES), out_dtype),
        scratch_shapes=[pltpu.VMEM((tm, d), BF16)],
        compiler_params=_cparams("parallel", "arbitrary"),
        name="norm_matmul",
    )(x2, gain.reshape(1, d), w_bf16)


def _block_products(f, t_idx, reverse):
    c = f.shape[0]
    e, fx, tot = f, jnp.ones_like(f), f
    levels = []
    b = 1
    while b < c:
        levels.append((e, fx))
        later = (t_idx & b) != 0
        if reverse:
            later = jnp.logical_not(later)
        back, fwd = (c - b, b) if reverse else (b, c - b)
        prev = pltpu.roll(tot, back, 0)
        nxt = pltpu.roll(tot, fwd, 0)
        e = e * jnp.where(later, prev, 1.0)
        fx = fx * jnp.where(later, 1.0, nxt)
        tot = tot * jnp.where(later, prev, nxt)
        b *= 2
    levels.append((e, fx))
    return levels, tot


def _gate_parts(z, lb):
    t = jnp.exp(-jnp.abs(z))
    r = 1.0 / (1.0 + t)
    pos = z >= 0.0
    sig = jnp.where(pos, r, t * r)
    sig_neg = jnp.where(pos, t * r, r)
    f = jnp.maximum(lb + (1.0 - lb) * sig, F_FLOOR)
    k = (1.0 - lb) * sig_neg
    return f, k


def _hgrn2_kernel(zq_ref, zi_ref, zg_ref, zff_ref, zfb_ref, lb_ref, gn_ref, o_ref,
                  oi_ref, qf_ref, qb_ref, kvf_ref, kvb_ref, df_ref, db_ref, sf_ref, sb_ref):
    c = HG_CHUNK
    nc = zq_ref.shape[0] // c
    nt = (((1,), (1,)), ((), ()))
    tn = (((0,), (0,)), ((), ()))
    lb_f = lb_ref[0:1, :]
    lb_b = lb_ref[1:2, :]
    t_idx = lax.broadcasted_iota(jnp.int32, (c, LANES), 0)
    pair_xor = (lax.broadcasted_iota(jnp.int32, (c, c), 0)
                ^ lax.broadcasted_iota(jnp.int32, (c, c), 1))

    def intra(ci, carry):
        rows = pl.ds(pl.multiple_of(ci * c, c), c)
        zq = zq_ref[rows, :].astype(F32)
        q = zq * _sigmoid(zq) * (HG_DIM ** -0.5)
        v = zi_ref[rows, :]
        f_f, k_f = _gate_parts(zff_ref[rows, :], lb_f)
        f_b, k_b = _gate_parts(zfb_ref[rows, :], lb_b)
        lev_f, tot_f = _block_products(f_f, t_idx, False)
        lev_b, tot_b = _block_products(f_b, t_idx, True)

        scores = None
        b = c // 2
        while b >= 1:
            li = b.bit_length() - 1
            right = (t_idx & b) != 0
            qh = q * jnp.where(right, lev_f[li][0], lev_b[li][0])
            kh = jnp.where(right, k_b * lev_b[li][1], k_f * lev_f[li][1])
            r = lax.dot_general(qh.astype(BF16), kh.astype(BF16), nt, preferred_element_type=F32)
            scores = r if scores is None else jnp.where(pair_xor < 2 * b, r, scores)
            b //= 2
        r = lax.dot_general(q.astype(BF16), (k_f + k_b).astype(BF16), nt,
                            preferred_element_type=F32)
        scores = jnp.where(pair_xor < 1, r, scores)
        oi_ref[rows, :] = jnp.dot(scores.astype(BF16), v, preferred_element_type=F32)

        e_f, x_f = lev_f[-1]
        e_b, x_b = lev_b[-1]
        qf_ref[rows, :] = (q * e_f).astype(BF16)
        qb_ref[rows, :] = (q * e_b).astype(BF16)
        kvf_ref[ci] = lax.dot_general(v, (k_f * x_f).astype(BF16), tn, preferred_element_type=F32)
        kvb_ref[ci] = lax.dot_general(v, (k_b * x_b).astype(BF16), tn, preferred_element_type=F32)
        df_ref[ci] = tot_f[0:SUBLANES, :]
        db_ref[ci] = tot_b[0:SUBLANES, :]
        return carry

    lax.fori_loop(0, nc, intra, 0)

    def scan_states(i, carry):
        s_f, s_b = carry
        cb = nc - 1 - i
        sf_ref[i] = s_f.astype(BF16)
        sb_ref[cb] = s_b.astype(BF16)
        s_f = s_f * df_ref[i][0:1, :] + kvf_ref[i]
        s_b = s_b * db_ref[cb][0:1, :] + kvb_ref[cb]
        return s_f, s_b

    zero = jnp.zeros((HG_DIM, HG_DIM), F32)
    lax.fori_loop(0, nc, scan_states, (zero, zero))

    gn = gn_ref[...]

    def finish(ci, carry):
        rows = pl.ds(pl.multiple_of(ci * c, c), c)
        o = oi_ref[rows, :]
        o = o + lax.dot_general(qf_ref[rows, :], sf_ref[ci], nt, preferred_element_type=F32)
        o = o + lax.dot_general(qb_ref[rows, :], sb_ref[ci], nt, preferred_element_type=F32)
        ms = jnp.mean(o * o, axis=-1, keepdims=True)
        o = o * lax.rsqrt(ms + EPS) * gn
        zg = zg_ref[rows, :].astype(F32)
        o_ref[rows, :] = (o * (zg * _sigmoid(zg))).astype(o_ref.dtype)
        return carry

    lax.fori_loop(0, nc, finish, 0)


def _hgrn2(z3, zf3, lb, gnorm, batch, seq):
    c = HG_CHUNK
    nc = seq // c
    blk = lambda cb0: pl.BlockSpec((None, seq, LANES), lambda b, h: (cb0 + h, b, 0))
    return pl.pallas_call(
        _hgrn2_kernel,
        grid=(batch, HG_HEADS),
        in_specs=[
            blk(CB_HQ), blk(CB_HI), blk(CB_HG),
            blk(0), blk(HG_HEADS),
            pl.BlockSpec((2, LANES), lambda b, h: (0, h)),
            pl.BlockSpec((1, LANES), lambda b, h: (0, 0)),
        ],
        out_specs=pl.BlockSpec((None, seq, LANES), lambda b, h: (h, b, 0)),
        out_shape=jax.ShapeDtypeStruct((HG_HEADS, batch * seq, LANES), BF16),
        scratch_shapes=[
            pltpu.VMEM((seq, LANES), F32),
            pltpu.VMEM((seq, LANES), BF16),
            pltpu.VMEM((seq, LANES), BF16),
            pltpu.VMEM((nc, HG_DIM, HG_DIM), F32),
            pltpu.VMEM((nc, HG_DIM, HG_DIM), F32),
            pltpu.VMEM((nc, SUBLANES, LANES), F32),
            pltpu.VMEM((nc, SUBLANES, LANES), F32),
            pltpu.VMEM((nc, HG_DIM, HG_DIM), BF16),
            pltpu.VMEM((nc, HG_DIM, HG_DIM), BF16),
        ],
        compiler_params=_cparams("parallel", "parallel"),
        name="hgrn2",
    )(z3, z3, z3, zf3, zf3, lb, gnorm.reshape(1, LANES))


def _natten_kernel(q_ref, k_ref, v_ref, bias_ref, o_ref):
    rows = q_ref.shape[0] // GRID_W
    kh = min(NA_KH, rows)
    win = kh * GRID_W
    nt = (((1,), (1,)), ((), ()))
    lane = lax.broadcasted_iota(jnp.int32, (GRID_W, LANES), 1)
    first = lane < NA_DIM

    def body(r, carry):
        rs = jnp.clip(r - kh // 2, 0, rows - kh)
        case = r - rs
        q2 = q_ref[pl.ds(pl.multiple_of(r * GRID_W, GRID_W), GRID_W), :] * (NA_DIM ** -0.5)
        kw = k_ref[pl.ds(pl.multiple_of(rs * GRID_W, GRID_W), win), :]
        vw = v_ref[pl.ds(pl.multiple_of(rs * GRID_W, GRID_W), win), :]
        outs = []
        for hh in range(2):
            qh = jnp.where(first if hh == 0 else jnp.logical_not(first), q2, jnp.zeros_like(q2))
            s = lax.dot_general(qh, kw, nt, preferred_element_type=F32)
            bm = bias_ref[hh, case]
            s = jnp.where(bm > 0.5 * MASK_NEG, s + bm, MASK_NEG)
            m = jnp.max(s, axis=-1, keepdims=True)
            p = jnp.exp(s - m)
            l = jnp.sum(p, axis=-1, keepdims=True)
            outs.append(jnp.dot(p.astype(BF16), vw, preferred_element_type=F32) / l)
        o = jnp.where(first, outs[0], outs[1])
        o_ref[pl.ds(pl.multiple_of(r * GRID_W, GRID_W), GRID_W), :] = o.astype(o_ref.dtype)
        return carry

    lax.fori_loop(0, rows, body, 0)


def _natten_bias_table(rpb, rows):
    kh = min(NA_KH, rows)
    r = np.arange(rows)
    rs = np.clip(r - kh // 2, 0, rows - kh)
    n_case = int((r - rs).max()) + 1
    case = np.arange(n_case)
    dr = np.arange(kh)[None, :] - case[:, None] + (NA_KH - 1)
    c = np.arange(GRID_W)
    col_start = np.clip(c - NA_KW // 2, 0, GRID_W - NA_KW)
    col_mask = (c[None, :] >= col_start[:, None]) & (c[None, :] < col_start[:, None] + NA_KW)
    dc = np.clip(c[None, :] - c[:, None], -(NA_KW - 1), NA_KW - 1) + (NA_KW - 1)
    tab = rpb[:, dr[:, None, :, None], dc[None, :, None, :]].astype(F32)
    tab = jnp.where(col_mask[None, None, :, None, :], tab, MASK_NEG)
    return tab.reshape(rpb.shape[0], n_case, GRID_W, kh * GRID_W)


def _natten(z3, bias_tab, batch, seq):
    n_case, win = bias_tab.shape[1], bias_tab.shape[3]
    blk = lambda cb0: pl.BlockSpec((None, seq, LANES), lambda b, p: (cb0 + p, b, 0))
    return pl.pallas_call(
        _natten_kernel,
        grid=(batch, NA_HEADS // 2),
        in_specs=[
            blk(CB_NQ), blk(CB_NK), blk(CB_NV),
            pl.BlockSpec((2, n_case, GRID_W, win), lambda b, p: (p, 0, 0, 0)),
        ],
        out_specs=pl.BlockSpec((None, seq, LANES), lambda b, p: (p, b, 0)),
        out_shape=jax.ShapeDtypeStruct((NA_HEADS // 2, batch * seq, LANES), BF16),
        compiler_params=_cparams("parallel", "parallel"),
        name="natten",
    )(z3, z3, z3, bias_tab)


def _memattn_kernel(q_ref, kv_ref, o_ref):
    nt = (((1,), (1,)), ((), ()))
    for h in range(CA_HEADS):
        s = lax.dot_general(q_ref[h], kv_ref[h], nt, preferred_element_type=F32) * (CA_DIM ** -0.5)
        m = jnp.max(s, axis=-1, keepdims=True)
        p = jnp.exp(s - m)
        l = jnp.sum(p, axis=-1, keepdims=True)
        o = jnp.dot(p.astype(BF16), kv_ref[CA_HEADS + h], preferred_element_type=F32) / l
        o_ref[h] = o.astype(o_ref.dtype)


def _memattn(z3, kv3, batch, seq, mem_len, tq):
    nq = seq // tq
    return pl.pallas_call(
        _memattn_kernel,
        grid=(batch, nq),
        in_specs=[
            pl.BlockSpec((CA_HEADS, tq, LANES), lambda b, i: (CB_CQ // CA_HEADS, b * nq + i, 0)),
            pl.BlockSpec((2 * CA_HEADS, mem_len, LANES), lambda b, i: (0, b, 0)),
        ],
        out_specs=pl.BlockSpec((CA_HEADS, tq, LANES), lambda b, i: (0, b * nq + i, 0)),
        out_shape=jax.ShapeDtypeStruct((CA_HEADS, batch * seq, LANES), BF16),
        compiler_params=_cparams("parallel", "parallel"),
        name="memattn",
    )(z3, kv3)


def _cat(ref):
    return jnp.concatenate([ref[j] for j in range(ref.shape[0])], axis=-1)


def _merge_kernel(x_ref, ohg_ref, ona_ref, oca_ref, ghg_ref, gna_ref, gca_ref,
                  whg_ref, wna_ref, wca_ref, wout_ref, o_ref):
    def branch(o3_ref, w_ref, g_ref):
        y = jnp.dot(_cat(o3_ref), w_ref[...], preferred_element_type=F32)
        return _sigmoid(_cat(g_ref).astype(F32)) * y

    merged = branch(ohg_ref, whg_ref, ghg_ref)
    merged = merged + branch(ona_ref, wna_ref, gna_ref)
    merged = merged + branch(oca_ref, wca_ref, gca_ref)
    o_ref[...] = x_ref[...] + jnp.dot(merged.astype(BF16), wout_ref[...],
                                      preferred_element_type=F32)


def _merge(x2, ohg3, ona3, oca3, z3, w_hg_o, w_na_o, w_ca_o, w_out, tm):
    m, d = x2.shape
    ncb = d // LANES
    act = lambda n: pl.BlockSpec((n, tm, LANES), lambda i: (0, i, 0))
    gate = lambda cb0: pl.BlockSpec((ncb, tm, LANES), lambda i: (cb0 // ncb, i, 0))
    full = lambda a: pl.BlockSpec(a.shape, lambda i: (0, 0))
    return pl.pallas_call(
        _merge_kernel,
        grid=(m // tm,),
        in_specs=[
            pl.BlockSpec((tm, d), lambda i: (i, 0)),
            act(ohg3.shape[0]), act(ona3.shape[0]), act(oca3.shape[0]),
            gate(CB_GHG), gate(CB_GNA), gate(CB_GCA),
            full(w_hg_o), full(w_na_o), full(w_ca_o), full(w_out),
        ],
        out_specs=pl.BlockSpec((tm, d), lambda i: (i, 0)),
        out_shape=jax.ShapeDtypeStruct((m, d), F32),
        input_output_aliases={0: 0},
        compiler_params=_cparams("parallel"),
        name="merge",
    )(x2, ohg3, ona3, oca3, z3, z3, z3, w_hg_o, w_na_o, w_ca_o, w_out)


def _ffn_kernel(x_ref, xp_ref, xn_ref, gain_ref, wa_ref, wg_ref, cwa_ref, cwg_ref, cba_ref, cbg_ref,
                wd_ref, fin_ref, o_ref, h_ref, acc_ref, *, tiles_per_seq, final_norm):
    i = pl.program_id(0)
    j = pl.program_id(1)
    tm = x_ref.shape[0]
    halo = SUBLANES

    def normed(x):
        ms = jnp.mean(x * x, axis=-1, keepdims=True)
        return x * lax.rsqrt(ms + EPS) * gain_ref[...]

    @pl.when(j == 0)
    def _():
        keep_prev = (i % tiles_per_seq != 0).astype(F32)
        keep_next = (i % tiles_per_seq != tiles_per_seq - 1).astype(F32)
        h_ref[0:halo, :] = (normed(xp_ref[...]) * keep_prev).astype(BF16)
        h_ref[halo:halo + tm, :] = normed(x_ref[...]).astype(BF16)
        h_ref[halo + tm:, :] = (normed(xn_ref[...]) * keep_next).astype(BF16)
        acc_ref[...] = jnp.zeros_like(acc_ref)

    h = h_ref[...]

    def conv_proj(w_ref, cw_ref, cb_ref):
        u = jnp.dot(h, w_ref[...], preferred_element_type=F32)
        out = u[halo - 1:halo - 1 + tm] * cw_ref[0:1, :]
        out = out + u[halo:halo + tm] * cw_ref[1:2, :]
        out = out + u[halo + 1:halo + 1 + tm] * cw_ref[2:3, :]
        return out + cb_ref[...]

    a = conv_proj(wa_ref, cwa_ref, cba_ref)
    g = conv_proj(wg_ref, cwg_ref, cbg_ref)
    cdf = 0.5 * (1.0 + jnp.tanh(np.float32(np.sqrt(2 / np.pi)) * (a + 0.044715 * (a * a * a))))
    y = (a * cdf * g).astype(BF16)
    acc_ref[...] += jnp.dot(y, wd_ref[...], preferred_element_type=F32)

    @pl.when(j == pl.num_programs(1) - 1)
    def _():
        out = x_ref[...] + acc_ref[...]
        if final_norm:
            ms = jnp.mean(out * out, axis=-1, keepdims=True)
            out = out * lax.rsqrt(ms + EPS) * fin_ref[...]
        o_ref[...] = out


def _conv_ffn(x2, gain, w_up, conv_w, conv_b, w_down, fin_gain, seq, tm, tf, final_norm):
    m, d = x2.shape
    nf = D_FF // tf
    hb = tm // SUBLANES
    n_halo = m // SUBLANES
    kern = functools.partial(_ffn_kernel, tiles_per_seq=seq // tm, final_norm=final_norm)
    return pl.pallas_call(
        kern,
        grid=(m // tm, nf),
        in_specs=[
            pl.BlockSpec((tm, d), lambda i, j: (i, 0)),
            pl.BlockSpec((SUBLANES, d), lambda i, j: (jnp.maximum(i * hb - 1, 0), 0)),
            pl.BlockSpec((SUBLANES, d), lambda i, j: (jnp.minimum((i + 1) * hb, n_halo - 1), 0)),
            pl.BlockSpec((1, d), lambda i, j: (0, 0)),
            pl.BlockSpec((d, tf), lambda i, j: (0, j)),
            pl.BlockSpec((d, tf), lambda i, j: (0, nf + j)),
            pl.BlockSpec((CONV_W, tf), lambda i, j: (0, j)),
            pl.BlockSpec((CONV_W, tf), lambda i, j: (0, nf + j)),
            pl.BlockSpec((1, tf), lambda i, j: (0, j)),
            pl.BlockSpec((1, tf), lambda i, j: (0, nf + j)),
            pl.BlockSpec((tf, d), lambda i, j: (j, 0)),
            pl.BlockSpec((1, d), lambda i, j: (0, 0)),
        ],
        out_specs=pl.BlockSpec((tm, d), lambda i, j: (i, 0)),
        out_shape=jax.ShapeDtypeStruct((m, d), F32),
        scratch_shapes=[
            pltpu.VMEM((tm + 2 * SUBLANES, d), BF16),
            pltpu.VMEM((tm, d), F32),
        ],
        compiler_params=_cparams("parallel", "arbitrary"),
        name="conv_ffn",
    )(x2, x2, x2, gain.reshape(1, d), w_up, w_up, conv_w, conv_w,
      conv_b.reshape(1, -1), conv_b.reshape(1, -1), w_down, fin_gain.reshape(1, d))


def kernel(x, mem, norm_mix, w_in, hg_lb_logits, hg_gnorm, w_hg_o, na_rpb, w_na_o, mem_norm,
           w_mem_kv, w_ca_o, w_out, norm_ffn, w_up, conv_w, conv_b, w_down, norm_final):
    batch, seq, d = x.shape
    mem_len = mem.shape[1]
    depth = w_in.shape[0]
    assert d == D_MODEL and seq % GRID_W == 0 and seq % HG_CHUNK == 0
    hgw = HG_HEADS * HG_DIM

    p_lb = jax.nn.softmax(hg_lb_logits.astype(F32), axis=0)
    lower_bounds = jnp.clip(jnp.cumsum(p_lb, axis=0) - p_lb[0], 0.0, 1.0)

    w_in_main = jnp.concatenate([w_in[:, :, :2 * hgw], w_in[:, :, 4 * hgw:]], axis=-1).astype(BF16)
    w_in_forget = w_in[:, :, 2 * hgw:4 * hgw].astype(BF16)
    assert w_in_main.shape[-1] == N_CB * LANES

    x2 = x.reshape(batch * seq, d)
    mem2 = mem.reshape(batch * mem_len, d)
    bf = lambda a: a.astype(BF16)

    for l in range(depth):
        z3 = _norm_matmul(x2, norm_mix[l], w_in_main[l], BF16, tm=1024, tn=2048)
        zf3 = _norm_matmul(x2, norm_mix[l], w_in_forget[l], F32, tm=1024, tn=1024)
        kv3 = _norm_matmul(mem2, mem_norm, bf(w_mem_kv[l]), BF16, tm=batch * mem_len, tn=1024)

        ohg3 = _hgrn2(z3, zf3, lower_bounds[l], hg_gnorm[l], batch, seq)
        ona3 = _natten(z3, _natten_bias_table(na_rpb[l], seq // GRID_W), batch, seq)
        oca3 = _memattn(z3, kv3, batch, seq, mem_len, tq=512)

        x2 = _merge(x2, ohg3, ona3, oca3, z3, bf(w_hg_o[l]), bf(w_na_o[l]), bf(w_ca_o[l]),
                    bf(w_out[l]), tm=512)
        x2 = _conv_ffn(x2, norm_ffn[l], bf(w_up[l]), conv_w[l], conv_b[l], bf(w_down[l]),
                       norm_final, seq, tm=512, tf=256, final_norm=(l == depth - 1))
    return x2.reshape(batch, seq, d)
```

```python
import functools

import numpy as np
import jax
import jax.numpy as jnp
from jax import lax
from jax.experimental import pallas as pl
from jax.experimental.pallas import tpu as pltpu

D_MODEL = 1024
GRID_W = 64
HG_HEADS = 8
HG_DIM = 128
NA_HEADS = 8
NA_DIM = 64
NA_KH = 8
NA_KW = 16
CA_HEADS = 4
CA_DIM = 128
D_FF = 2816
CONV_W = 3
EPS = 1e-6
F_FLOOR = 1e-12
MASK_NEG = -1e30

LANES = 128
SUBLANES = 8
VMEM_LIMIT = 48 * 1024 * 1024

HG_CHUNK = 128

CB_HQ, CB_HI, CB_HG = 0, 8, 16
CB_NQ, CB_NK, CB_NV = 24, 28, 32
CB_CQ = 36
CB_GHG, CB_GNA, CB_GCA = 40, 48, 56
N_CB = 64

F32 = jnp.float32
BF16 = jnp.bfloat16


def _cparams(*sem):
    return pltpu.CompilerParams(dimension_semantics=sem, vmem_limit_bytes=VMEM_LIMIT)


def _sigmoid(x):
    return 1.0 / (1.0 + jnp.exp(-x))


def _norm_mm_kernel(x_ref, g_ref, w_ref, o_ref, h_ref):
    @pl.when(pl.program_id(1) == 0)
    def _():
        x = x_ref[...]
        ms = jnp.mean(x * x, axis=-1, keepdims=True)
        h_ref[...] = (x * lax.rsqrt(ms + EPS) * g_ref[...]).astype(BF16)

    acc = jnp.dot(h_ref[...], w_ref[...], preferred_element_type=F32)
    for j in range(o_ref.shape[0]):
        o_ref[j] = acc[:, j * LANES:(j + 1) * LANES].astype(o_ref.dtype)


def _norm_matmul(x2, gain, w_bf16, out_dtype, tm, tn):
    m, d = x2.shape
    n = w_bf16.shape[1]
    return pl.pallas_call(
        _norm_mm_kernel,
        grid=(m // tm, n // tn),
        in_specs=[
            pl.BlockSpec((tm, d), lambda i, j: (i, 0)),
            pl.BlockSpec((1, d), lambda i, j: (0, 0)),
            pl.BlockSpec((d, tn), lambda i, j: (0, j)),
        ],
        out_specs=pl.BlockSpec((tn // LANES, tm, LANES), lambda i, j: (j, i, 0)),
        out_shape=jax.ShapeDtypeStruct((n // LANES, m, LANES), out_dtype),
        scratch_shapes=[pltpu.VMEM((tm, d), BF16)],
        compiler_params=_cparams("parallel", "arbitrary"),
        name="norm_matmul",
    )(x2, gain.reshape(1, d), w_bf16)


def _block_products(f, t_idx, reverse):
    c = f.shape[0]
    e, fx, tot = f, jnp.ones_like(f), f
    levels = []
    b = 1
    while b < c:
        levels.append((e, fx))
        later = (t_idx & b) != 0
        if reverse:
            later = jnp.logical_not(later)
        back, fwd = (c - b, b) if reverse else (b, c - b)
        prev = pltpu.roll(tot, back, 0)
        nxt = pltpu.roll(tot, fwd, 0)
        e = e * jnp.where(later, prev, 1.0)
        fx = fx * jnp.where(later, 1.0, nxt)
        tot = tot * jnp.where(later, prev, nxt)
        b *= 2
    levels.append((e, fx))
    return levels, tot


def _gate_parts(z, lb):
    t = jnp.exp(-jnp.abs(z))
    r = 1.0 / (1.0 + t)
    pos = z >= 0.0
    sig = jnp.where(pos, r, t * r)
    sig_neg = jnp.where(pos, t * r, r)
    f = jnp.maximum(lb + (1.0 - lb) * sig, F_FLOOR)
    k = (1.0 - lb) * sig_neg
    return f, k


def _hgrn2_kernel(zq_ref, zi_ref, zg_ref, zff_ref, zfb_ref, lb_ref, gn_ref, o_ref,
                  oi_ref, qf_ref, qb_ref, kvf_ref, kvb_ref, df_ref, db_ref, sf_ref, sb_ref):
    c = HG_CHUNK
    nc = zq_ref.shape[0] // c
    nt = (((1,), (1,)), ((), ()))
    tn = (((0,), (0,)), ((), ()))
    lb_f = lb_ref[0:1, :]
    lb_b = lb_ref[1:2, :]
    t_idx = lax.broadcasted_iota(jnp.int32, (c, LANES), 0)
    pair_xor = (lax.broadcasted_iota(jnp.int32, (c, c), 0)
                ^ lax.broadcasted_iota(jnp.int32, (c, c), 1))

    def intra(ci, carry):
        rows = pl.ds(pl.multiple_of(ci * c, c), c)
        zq = zq_ref[rows, :].astype(F32)
        q = zq * _sigmoid(zq) * (HG_DIM ** -0.5)
        v = zi_ref[rows, :]
        f_f, k_f = _gate_parts(zff_ref[rows, :], lb_f)
        f_b, k_b = _gate_parts(zfb_ref[rows, :], lb_b)
        lev_f, tot_f = _block_products(f_f, t_idx, False)
        lev_b, tot_b = _block_products(f_b, t_idx, True)

        scores = None
        b = c // 2
        while b >= 1:
            li = b.bit_length() - 1
            right = (t_idx & b) != 0
            qh = q * jnp.where(right, lev_f[li][0], lev_b[li][0])
            kh = jnp.where(right, k_b * lev_b[li][1], k_f * lev_f[li][1])
            r = lax.dot_general(qh.astype(BF16), kh.astype(BF16), nt, preferred_element_type=F32)
            scores = r if scores is None else jnp.where(pair_xor < 2 * b, r, scores)
            b //= 2
        r = lax.dot_general(q.astype(BF16), (k_f + k_b).astype(BF16), nt,
                            preferred_element_type=F32)
        scores = jnp.where(pair_xor < 1, r, scores)
        oi_ref[rows, :] = jnp.dot(scores.astype(BF16), v, preferred_element_type=F32)

        e_f, x_f = lev_f[-1]
        e_b, x_b = lev_b[-1]
        qf_ref[rows, :] = (q * e_f).astype(BF16)
        qb_ref[rows, :] = (q * e_b).astype(BF16)
        kvf_ref[ci] = lax.dot_general(v, (k_f * x_f).astype(BF16), tn, preferred_element_type=F32)
        kvb_ref[ci] = lax.dot_general(v, (k_b * x_b).astype(BF16), tn, preferred_element_type=F32)
        df_ref[ci] = tot_f[0:SUBLANES, :]
        db_ref[ci] = tot_b[0:SUBLANES, :]
        return carry

    lax.fori_loop(0, nc, intra, 0)

    def scan_states(i, carry):
        s_f, s_b = carry
        cb = nc - 1 - i
        sf_ref[i] = s_f.astype(BF16)
        sb_ref[cb] = s_b.astype(BF16)
        s_f = s_f * df_ref[i][0:1, :] + kvf_ref[i]
        s_b = s_b * db_ref[cb][0:1, :] + kvb_ref[cb]
        return s_f, s_b

    zero = jnp.zeros((HG_DIM, HG_DIM), F32)
    lax.fori_loop(0, nc, scan_states, (zero, zero))

    gn = gn_ref[...]

    def finish(ci, carry):
        rows = pl.ds(pl.multiple_of(ci * c, c), c)
        o = oi_ref[rows, :]
        o = o + lax.dot_general(qf_ref[rows, :], sf_ref[ci], nt, preferred_element_type=F32)
        o = o + lax.dot_general(qb_ref[rows, :], sb_ref[ci], nt, preferred_element_type=F32)
        ms = jnp.mean(o * o, axis=-1, keepdims=True)
        o = o * lax.rsqrt(ms + EPS) * gn
        zg = zg_ref[rows, :].astype(F32)
        o_ref[rows, :] = (o * (zg * _sigmoid(zg))).astype(o_ref.dtype)
        return carry

    lax.fori_loop(0, nc, finish, 0)


def _hgrn2(z3, zf3, lb, gnorm, batch, seq):
    c = HG_CHUNK
    nc = seq // c
    blk = lambda cb0: pl.BlockSpec((None, seq, LANES), lambda b, h: (cb0 + h, b, 0))
    return pl.pallas_call(
        _hgrn2_kernel,
        grid=(batch, HG_HEADS),
        in_specs=[
            blk(CB_HQ), blk(CB_HI), blk(CB_HG),
            blk(0), blk(HG_HEADS),
            pl.BlockSpec((2, LANES), lambda b, h: (0, h)),
            pl.BlockSpec((1, LANES), lambda b, h: (0, 0)),
        ],
        out_specs=pl.BlockSpec((None, seq, LANES), lambda b, h: (h, b, 0)),
        out_shape=jax.ShapeDtypeStruct((HG_HEADS, batch * seq, LANES), BF16),
        scratch_shapes=[
            pltpu.VMEM((seq, LANES), F32),
            pltpu.VMEM((seq, LANES), BF16),
            pltpu.VMEM((seq, LANES), BF16),
            pltpu.VMEM((nc, HG_DIM, HG_DIM), F32),
            pltpu.VMEM((nc, HG_DIM, HG_DIM), F32),
            pltpu.VMEM((nc, SUBLANES, LANES), F32),
            pltpu.VMEM((nc, SUBLANES, LANES), F32),
            pltpu.VMEM((nc, HG_DIM, HG_DIM), BF16),
            pltpu.VMEM((nc, HG_DIM, HG_DIM), BF16),
        ],
        compiler_params=_cparams("parallel", "parallel"),
        name="hgrn2",
    )(z3, z3, z3, zf3, zf3, lb, gnorm.reshape(1, LANES))


def _natten_kernel(q_ref, k_ref, v_ref, bias_ref, o_ref):
    rows = q_ref.shape[0] // GRID_W
    kh = min(NA_KH, rows)
    win = kh * GRID_W
    nt = (((1,), (1,)), ((), ()))
    lane = lax.broadcasted_iota(jnp.int32, (GRID_W, LANES), 1)
    first = lane < NA_DIM

    def body(r, carry):
        rs = jnp.clip(r - kh // 2, 0, rows - kh)
        case = r - rs
        q2 = q_ref[pl.ds(pl.multiple_of(r * GRID_W, GRID_W), GRID_W), :] * (NA_DIM ** -0.5)
        kw = k_ref[pl.ds(pl.multiple_of(rs * GRID_W, GRID_W), win), :]
        vw = v_ref[pl.ds(pl.multiple_of(rs * GRID_W, GRID_W), win), :]
        outs = []
        for hh in range(2):
            qh = jnp.where(first if hh == 0 else jnp.logical_not(first), q2, jnp.zeros_like(q2))
            s = lax.dot_general(qh, kw, nt, preferred_element_type=F32)
            bm = bias_ref[hh, case]
            s = jnp.where(bm > 0.5 * MASK_NEG, s + bm, MASK_NEG)
            m = jnp.max(s, axis=-1, keepdims=True)
            p = jnp.exp(s - m)
            l = jnp.sum(p, axis=-1, keepdims=True)
            outs.append(jnp.dot(p.astype(BF16), vw, preferred_element_type=F32) / l)
        o = jnp.where(first, outs[0], outs[1])
        o_ref[pl.ds(pl.multiple_of(r * GRID_W, GRID_W), GRID_W), :] = o.astype(o_ref.dtype)
        return carry

    lax.fori_loop(0, rows, body, 0)


def _natten_bias_table(rpb, rows):
    kh = min(NA_KH, rows)
    r = np.arange(rows)
    rs = np.clip(r - kh // 2, 0, rows - kh)
    n_case = int((r - rs).max()) + 1
    case = np.arange(n_case)
    dr = np.arange(kh)[None, :] - case[:, None] + (NA_KH - 1)
    c = np.arange(GRID_W)
    col_start = np.clip(c - NA_KW // 2, 0, GRID_W - NA_KW)
    col_mask = (c[None, :] >= col_start[:, None]) & (c[None, :] < col_start[:, None] + NA_KW)
    dc = np.clip(c[None, :] - c[:, None], -(NA_KW - 1), NA_KW - 1) + (NA_KW - 1)
    by_row = jnp.take(rpb.astype(F32), dr.reshape(-1), axis=1)
    by_row = by_row.reshape(rpb.shape[0], n_case, 1, kh, 1, 2 * NA_KW - 1)
    shape = (rpb.shape[0], n_case, GRID_W, kh, GRID_W)
    tab = jnp.full(shape, MASK_NEG, F32)
    for d in range(2 * NA_KW - 1):
        hit = ((dc == d) & col_mask)[None, None, :, None, :]
        tab = jnp.where(hit, by_row[..., d], tab)
    return tab.reshape(rpb.shape[0], n_case, GRID_W, kh * GRID_W)


def _natten(z3, bias_tab, batch, seq):
    n_case, win = bias_tab.shape[1], bias_tab.shape[3]
    blk = lambda cb0: pl.BlockSpec((None, seq, LANES), lambda b, p: (cb0 + p, b, 0))
    return pl.pallas_call(
        _natten_kernel,
        grid=(batch, NA_HEADS // 2),
        in_specs=[
            blk(CB_NQ), blk(CB_NK), blk(CB_NV),
            pl.BlockSpec((2, n_case, GRID_W, win), lambda b, p: (p, 0, 0, 0)),
        ],
        out_specs=pl.BlockSpec((None, seq, LANES), lambda b, p: (p, b, 0)),
        out_shape=jax.ShapeDtypeStruct((NA_HEADS // 2, batch * seq, LANES), BF16),
        compiler_params=_cparams("parallel", "parallel"),
        name="natten",
    )(z3, z3, z3, bias_tab)


def _memattn_kernel(q_ref, kv_ref, o_ref):
    nt = (((1,), (1,)), ((), ()))
    for h in range(CA_HEADS):
        s = lax.dot_general(q_ref[h], kv_ref[h], nt, preferred_element_type=F32) * (CA_DIM ** -0.5)
        m = jnp.max(s, axis=-1, keepdims=True)
        p = jnp.exp(s - m)
        l = jnp.sum(p, axis=-1, keepdims=True)
        o = jnp.dot(p.astype(BF16), kv_ref[CA_HEADS + h], preferred_element_type=F32) / l
        o_ref[h] = o.astype(o_ref.dtype)


def _memattn(z3, kv3, batch, seq, mem_len, tq):
    nq = seq // tq
    return pl.pallas_call(
        _memattn_kernel,
        grid=(batch, nq),
        in_specs=[
            pl.BlockSpec((CA_HEADS, tq, LANES), lambda b, i: (CB_CQ // CA_HEADS, b * nq + i, 0)),
            pl.BlockSpec((2 * CA_HEADS, mem_len, LANES), lambda b, i: (0, b, 0)),
        ],
        out_specs=pl.BlockSpec((CA_HEADS, tq, LANES), lambda b, i: (0, b * nq + i, 0)),
        out_shape=jax.ShapeDtypeStruct((CA_HEADS, batch * seq, LANES), BF16),
        compiler_params=_cparams("parallel", "parallel"),
        name="memattn",
    )(z3, kv3)


def _cat(ref):
    return jnp.concatenate([ref[j] for j in range(ref.shape[0])], axis=-1)


def _merge_kernel(x_ref, ohg_ref, ona_ref, oca_ref, ghg_ref, gna_ref, gca_ref,
                  whg_ref, wna_ref, wca_ref, wout_ref, o_ref):
    def branch(o3_ref, w_ref, g_ref):
        y = jnp.dot(_cat(o3_ref), w_ref[...], preferred_element_type=F32)
        return _sigmoid(_cat(g_ref).astype(F32)) * y

    merged = branch(ohg_ref, whg_ref, ghg_ref)
    merged = merged + branch(ona_ref, wna_ref, gna_ref)
    merged = merged + branch(oca_ref, wca_ref, gca_ref)
    o_ref[...] = x_ref[...] + jnp.dot(merged.astype(BF16), wout_ref[...],
                                      preferred_element_type=F32)


def _merge(x2, ohg3, ona3, oca3, z3, w_hg_o, w_na_o, w_ca_o, w_out, tm):
    m, d = x2.shape
    ncb = d // LANES
    act = lambda n: pl.BlockSpec((n, tm, LANES), lambda i: (0, i, 0))
    gate = lambda cb0: pl.BlockSpec((ncb, tm, LANES), lambda i: (cb0 // ncb, i, 0))
    full = lambda a: pl.BlockSpec(a.shape, lambda i: (0, 0))
    return pl.pallas_call(
        _merge_kernel,
        grid=(m // tm,),
        in_specs=[
            pl.BlockSpec((tm, d), lambda i: (i, 0)),
            act(ohg3.shape[0]), act(ona3.shape[0]), act(oca3.shape[0]),
            gate(CB_GHG), gate(CB_GNA), gate(CB_GCA),
            full(w_hg_o), full(w_na_o), full(w_ca_o), full(w_out),
        ],
        out_specs=pl.BlockSpec((tm, d), lambda i: (i, 0)),
        out_shape=jax.ShapeDtypeStruct((m, d), F32),
        input_output_aliases={0: 0},
        compiler_params=_cparams("parallel"),
        name="merge",
    )(x2, ohg3, ona3, oca3, z3, z3, z3, w_hg_o, w_na_o, w_ca_o, w_out)


def _ffn_kernel(x_ref, xp_ref, xn_ref, gain_ref, wa_ref, wg_ref, cwa_ref, cwg_ref, cba_ref, cbg_ref,
                wd_ref, fin_ref, o_ref, h_ref, acc_ref, *, tiles_per_seq, final_norm):
    i = pl.program_id(0)
    j = pl.program_id(1)
    tm = x_ref.shape[0]
    halo = SUBLANES

    def normed(x):
        ms = jnp.mean(x * x, axis=-1, keepdims=True)
        return x * lax.rsqrt(ms + EPS) * gain_ref[...]

    @pl.when(j == 0)
    def _():
        keep_prev = (i % tiles_per_seq != 0).astype(F32)
        keep_next = (i % tiles_per_seq != tiles_per_seq - 1).astype(F32)
        h_ref[0:halo, :] = (normed(xp_ref[...]) * keep_prev).astype(BF16)
        h_ref[halo:halo + tm, :] = normed(x_ref[...]).astype(BF16)
        h_ref[halo + tm:, :] = (normed(xn_ref[...]) * keep_next).astype(BF16)
        acc_ref[...] = jnp.zeros_like(acc_ref)

    h = h_ref[...]

    def conv_proj(w_ref, cw_ref, cb_ref):
        u = jnp.dot(h, w_ref[...], preferred_element_type=F32)
        out = u[halo - 1:halo - 1 + tm] * cw_ref[0:1, :]
        out = out + u[halo:halo + tm] * cw_ref[1:2, :]
        out = out + u[halo + 1:halo + 1 + tm] * cw_ref[2:3, :]
        return out + cb_ref[...]

    a = conv_proj(wa_ref, cwa_ref, cba_ref)
    g = conv_proj(wg_ref, cwg_ref, cbg_ref)
    cdf = 0.5 * (1.0 + jnp.tanh(np.float32(np.sqrt(2 / np.pi)) * (a + 0.044715 * (a * a * a))))
    y = (a * cdf * g).astype(BF16)
    acc_ref[...] += jnp.dot(y, wd_ref[...], preferred_element_type=F32)

    @pl.when(j == pl.num_programs(1) - 1)
    def _():
        out = x_ref[...] + acc_ref[...]
        if final_norm:
            ms = jnp.mean(out * out, axis=-1, keepdims=True)
            out = out * lax.rsqrt(ms + EPS) * fin_ref[...]
        o_ref[...] = out


def _conv_ffn(x2, gain, w_up, conv_w, conv_b, w_down, fin_gain, seq, tm, tf, final_norm):
    m, d = x2.shape
    nf = D_FF // tf
    hb = tm // SUBLANES
    n_halo = m // SUBLANES
    kern = functools.partial(_ffn_kernel, tiles_per_seq=seq // tm, final_norm=final_norm)
    return pl.pallas_call(
        kern,
        grid=(m // tm, nf),
        in_specs=[
            pl.BlockSpec((tm, d), lambda i, j: (i, 0)),
            pl.BlockSpec((SUBLANES, d), lambda i, j: (jnp.maximum(i * hb - 1, 0), 0)),
            pl.BlockSpec((SUBLANES, d), lambda i, j: (jnp.minimum((i + 1) * hb, n_halo - 1), 0)),
            pl.BlockSpec((1, d), lambda i, j: (0, 0)),
            pl.BlockSpec((d, tf), lambda i, j: (0, j)),
            pl.BlockSpec((d, tf), lambda i, j: (0, nf + j)),
            pl.BlockSpec((CONV_W, tf), lambda i, j: (0, j)),
            pl.BlockSpec((CONV_W, tf), lambda i, j: (0, nf + j)),
            pl.BlockSpec((1, tf), lambda i, j: (0, j)),
            pl.BlockSpec((1, tf), lambda i, j: (0, nf + j)),
            pl.BlockSpec((tf, d), lambda i, j: (j, 0)),
            pl.BlockSpec((1, d), lambda i, j: (0, 0)),
        ],
        out_specs=pl.BlockSpec((tm, d), lambda i, j: (i, 0)),
        out_shape=jax.ShapeDtypeStruct((m, d), F32),
        scratch_shapes=[
            pltpu.VMEM((tm + 2 * SUBLANES, d), BF16),
            pltpu.VMEM((tm, d), F32),
        ],
        compiler_params=_cparams("parallel", "arbitrary"),
        name="conv_ffn",
    )(x2, x2, x2, gain.reshape(1, d), w_up, w_up, conv_w, conv_w,
      conv_b.reshape(1, -1), conv_b.reshape(1, -1), w_down, fin_gain.reshape(1, d))


def kernel(x, mem, norm_mix, w_in, hg_lb_logits, hg_gnorm, w_hg_o, na_rpb, w_na_o, mem_norm,
           w_mem_kv, w_ca_o, w_out, norm_ffn, w_up, conv_w, conv_b, w_down, norm_final):
    batch, seq, d = x.shape
    mem_len = mem.shape[1]
    depth = w_in.shape[0]
    assert d == D_MODEL and seq % GRID_W == 0 and seq % HG_CHUNK == 0
    hgw = HG_HEADS * HG_DIM

    p_lb = jax.nn.softmax(hg_lb_logits.astype(F32), axis=0)
    lower_bounds = jnp.clip(jnp.cumsum(p_lb, axis=0) - p_lb[0], 0.0, 1.0)

    w_in_main = jnp.concatenate([w_in[:, :, :2 * hgw], w_in[:, :, 4 * hgw:]], axis=-1).astype(BF16)
    w_in_forget = w_in[:, :, 2 * hgw:4 * hgw].astype(BF16)
    assert w_in_main.shape[-1] == N_CB * LANES

    x2 = x.reshape(batch * seq, d)
    mem2 = mem.reshape(batch * mem_len, d)
    bf = lambda a: a.astype(BF16)

    for l in range(depth):
        z3 = _norm_matmul(x2, norm_mix[l], w_in_main[l], BF16, tm=1024, tn=2048)
        zf3 = _norm_matmul(x2, norm_mix[l], w_in_forget[l], F32, tm=1024, tn=1024)
        kv3 = _norm_matmul(mem2, mem_norm, bf(w_mem_kv[l]), BF16, tm=batch * mem_len, tn=1024)

        ohg3 = _hgrn2(z3, zf3, lower_bounds[l], hg_gnorm[l], batch, seq)
        ona3 = _natten(z3, _natten_bias_table(na_rpb[l], seq // GRID_W), batch, seq)
        oca3 = _memattn(z3, kv3, batch, seq, mem_len, tq=512)

        x2 = _merge(x2, ohg3, ona3, oca3, z3, bf(w_hg_o[l]), bf(w_na_o[l]), bf(w_ca_o[l]),
                    bf(w_out[l]), tm=512)
        x2 = _conv_ffn(x2, norm_ffn[l], bf(w_up[l]), conv_w[l], conv_b[l], bf(w_down[l]),
                       norm_final, seq, tm=512, tf=256, final_norm=(l == depth - 1))
    return x2.reshape(batch, seq, d)
```

```python
import functools

import numpy as np
import jax
import jax.numpy as jnp
from jax import lax
from jax.experimental import pallas as pl
from jax.experimental.pallas import tpu as pltpu

D_MODEL = 1024
GRID_W = 64
HG_HEADS = 8
HG_DIM = 128
NA_HEADS = 8
NA_DIM = 64
NA_KH = 8
NA_KW = 16
CA_HEADS = 4
CA_DIM = 128
D_FF = 2816
CONV_W = 3
EPS = 1e-6
F_FLOOR = 1e-12
MASK_NEG = -1e30

LANES = 128
SUBLANES = 8
VMEM_LIMIT = 48 * 1024 * 1024

HG_CHUNK = 128
NA_GROUP = 4

CB_HQ, CB_HI, CB_HG = 0, 8, 16
CB_NQ, CB_NK, CB_NV = 24, 28, 32
CB_CQ = 36
CB_GHG, CB_GNA, CB_GCA = 40, 48, 56
N_CB = 64

F32 = jnp.float32
BF16 = jnp.bfloat16


def _cparams(*sem):
    return pltpu.CompilerParams(dimension_semantics=sem, vmem_limit_bytes=VMEM_LIMIT)


def _sigmoid(x):
    return 1.0 / (1.0 + jnp.exp(-x))


def _norm_mm_kernel(x_ref, g_ref, w_ref, o_ref, h_ref):
    @pl.when(pl.program_id(1) == 0)
    def _():
        x = x_ref[...]
        ms = jnp.mean(x * x, axis=-1, keepdims=True)
        h_ref[...] = (x * lax.rsqrt(ms + EPS) * g_ref[...]).astype(BF16)

    acc = jnp.dot(h_ref[...], w_ref[...], preferred_element_type=F32)
    for j in range(o_ref.shape[0]):
        o_ref[j] = acc[:, j * LANES:(j + 1) * LANES].astype(o_ref.dtype)


def _norm_matmul(x2, gain, w_bf16, out_dtype, tm, tn):
    m, d = x2.shape
    n = w_bf16.shape[1]
    return pl.pallas_call(
        _norm_mm_kernel,
        grid=(m // tm, n // tn),
        in_specs=[
            pl.BlockSpec((tm, d), lambda i, j: (i, 0)),
            pl.BlockSpec((1, d), lambda i, j: (0, 0)),
            pl.BlockSpec((d, tn), lambda i, j: (0, j)),
        ],
        out_specs=pl.BlockSpec((tn // LANES, tm, LANES), lambda i, j: (j, i, 0)),
        out_shape=jax.ShapeDtypeStruct((n // LANES, m, LANES), out_dtype),
        scratch_shapes=[pltpu.VMEM((tm, d), BF16)],
        compiler_params=_cparams("parallel", "arbitrary"),
        name="norm_matmul",
    )(x2, gain.reshape(1, d), w_bf16)


def _slabs(x):
    return [x[i:i + SUBLANES, :] for i in range(0, x.shape[0], SUBLANES)]


def _join(slabs):
    return jnp.concatenate(slabs, axis=0)


def _mul(a, b):
    return b if a is None else a * b


def _block_products(f, row8, reverse):
    n = len(f)
    e, fx, tot = list(f), [None] * n, list(f)
    levels = {1: (e, fx)}
    b = 1
    while b < SUBLANES:
        later = (row8 & b) != 0
        if reverse:
            later = jnp.logical_not(later)
        back, fwd = (SUBLANES - b, b) if reverse else (b, SUBLANES - b)
        prev = [pltpu.roll(t, back, 0) for t in tot]
        nxt = [pltpu.roll(t, fwd, 0) for t in tot]
        e = [e[i] * jnp.where(later, prev[i], 1.0) for i in range(n)]
        fx = [_mul(fx[i], jnp.where(later, 1.0, nxt[i])) for i in range(n)]
        tot = [tot[i] * jnp.where(later, prev[i], nxt[i]) for i in range(n)]
        b *= 2
        levels[b] = (e, fx)
    m = 1
    while m < n:
        e, fx, nt = list(e), list(fx), []
        for p in range(0, n, 2 * m):
            first, second = (p + m, p) if reverse else (p, p + m)
            t_first, t_second = tot[first // m], tot[second // m]
            for i in range(m):
                e[second + i] = e[second + i] * t_first
                fx[first + i] = _mul(fx[first + i], t_second)
            nt.append(t_first * t_second)
        tot = nt
        m *= 2
        levels[m * SUBLANES] = (e, fx)
    return levels, tot[0]


def _gate_parts(z, lb):
    t = jnp.exp(-jnp.abs(z))
    r = 1.0 / (1.0 + t)
    pos = z >= 0.0
    sig = jnp.where(pos, r, t * r)
    sig_neg = jnp.where(pos, t * r, r)
    f = jnp.maximum(lb + (1.0 - lb) * sig, F_FLOOR)
    k = (1.0 - lb) * sig_neg
    return f, k


def _hgrn2_kernel(zq_ref, zi_ref, zg_ref, zff_ref, zfb_ref, lb_ref, gn_ref, o_ref,
                  oi_ref, qf_ref, qb_ref, kvf_ref, kvb_ref, df_ref, db_ref, sf_ref, sb_ref):
    c = HG_CHUNK
    nc = zq_ref.shape[0] // c
    nt = (((1,), (1,)), ((), ()))
    tn = (((0,), (0,)), ((), ()))
    lb_f = lb_ref[0:1, :]
    lb_b = lb_ref[1:2, :]
    row8 = lax.broadcasted_iota(jnp.int32, (SUBLANES, LANES), 0)
    pair_xor = (lax.broadcasted_iota(jnp.int32, (c, c), 0)
                ^ lax.broadcasted_iota(jnp.int32, (c, c), 1))
    n_slab = c // SUBLANES

    def intra(ci, carry):
        rows = pl.ds(pl.multiple_of(ci * c, c), c)
        zq = zq_ref[rows, :].astype(F32)
        q = _slabs(zq * _sigmoid(zq) * (HG_DIM ** -0.5))
        v = zi_ref[rows, :]
        f_f, k_f = _gate_parts(zff_ref[rows, :], lb_f)
        f_b, k_b = _gate_parts(zfb_ref[rows, :], lb_b)
        k_f, k_b = _slabs(k_f), _slabs(k_b)
        lev_f, tot_f = _block_products(_slabs(f_f), row8, False)
        lev_b, tot_b = _block_products(_slabs(f_b), row8, True)

        scores = None
        b = c // 2
        while b >= 1:
            (e_f, x_f), (e_b, x_b) = lev_f[b], lev_b[b]
            if b >= SUBLANES:
                is_right = [(i // (b // SUBLANES)) % 2 == 1 for i in range(n_slab)]
                qh = [q[i] * (e_f[i] if is_right[i] else e_b[i]) for i in range(n_slab)]
                kh = [_mul(x_b[i], k_b[i]) if is_right[i] else _mul(x_f[i], k_f[i])
                      for i in range(n_slab)]
            else:
                right = (row8 & b) != 0
                qh = [q[i] * jnp.where(right, e_f[i], e_b[i]) for i in range(n_slab)]
                kh = [jnp.where(right, _mul(x_b[i], k_b[i]), _mul(x_f[i], k_f[i]))
                      for i in range(n_slab)]
            r = lax.dot_general(_join(qh).astype(BF16), _join(kh).astype(BF16), nt,
                                preferred_element_type=F32)
            scores = r if scores is None else jnp.where(pair_xor < 2 * b, r, scores)
            b //= 2
        k_both = [k_f[i] + k_b[i] for i in range(n_slab)]
        r = lax.dot_general(_join(q).astype(BF16), _join(k_both).astype(BF16), nt,
                            preferred_element_type=F32)
        scores = jnp.where(pair_xor < 1, r, scores)
        oi_ref[rows, :] = jnp.dot(scores.astype(BF16), v, preferred_element_type=F32)

        (e_f, x_f), (e_b, x_b) = lev_f[c], lev_b[c]
        qf_ref[rows, :] = _join([q[i] * e_f[i] for i in range(n_slab)]).astype(BF16)
        qb_ref[rows, :] = _join([q[i] * e_b[i] for i in range(n_slab)]).astype(BF16)
        kd_f = _join([_mul(x_f[i], k_f[i]) for i in range(n_slab)]).astype(BF16)
        kd_b = _join([_mul(x_b[i], k_b[i]) for i in range(n_slab)]).astype(BF16)
        kvf_ref[ci] = lax.dot_general(v, kd_f, tn, preferred_element_type=F32)
        kvb_ref[ci] = lax.dot_general(v, kd_b, tn, preferred_element_type=F32)
        df_ref[ci] = tot_f
        db_ref[ci] = tot_b
        return carry

    lax.fori_loop(0, nc, intra, 0, unroll=2)

    def scan_states(i, carry):
        s_f, s_b = carry
        cb = nc - 1 - i
        sf_ref[i] = s_f.astype(BF16)
        sb_ref[cb] = s_b.astype(BF16)
        s_f = s_f * df_ref[i][0:1, :] + kvf_ref[i]
        s_b = s_b * db_ref[cb][0:1, :] + kvb_ref[cb]
        return s_f, s_b

    zero = jnp.zeros((HG_DIM, HG_DIM), F32)
    lax.fori_loop(0, nc, scan_states, (zero, zero))

    gn = gn_ref[...]

    def finish(ci, carry):
        rows = pl.ds(pl.multiple_of(ci * c, c), c)
        o = oi_ref[rows, :]
        o = o + lax.dot_general(qf_ref[rows, :], sf_ref[ci], nt, preferred_element_type=F32)
        o = o + lax.dot_general(qb_ref[rows, :], sb_ref[ci], nt, preferred_element_type=F32)
        ms = jnp.mean(o * o, axis=-1, keepdims=True)
        o = o * lax.rsqrt(ms + EPS) * gn
        zg = zg_ref[rows, :].astype(F32)
        o_ref[rows, :] = (o * (zg * _sigmoid(zg))).astype(o_ref.dtype)
        return carry

    lax.fori_loop(0, nc, finish, 0, unroll=4)


def _hgrn2(z3, zf3, lb, gnorm, batch, seq):
    c = HG_CHUNK
    nc = seq // c
    blk = lambda cb0: pl.BlockSpec((None, seq, LANES), lambda b, h: (cb0 + h, b, 0))
    return pl.pallas_call(
        _hgrn2_kernel,
        grid=(batch, HG_HEADS),
        in_specs=[
            blk(CB_HQ), blk(CB_HI), blk(CB_HG),
            blk(0), blk(HG_HEADS),
            pl.BlockSpec((2, LANES), lambda b, h: (0, h)),
            pl.BlockSpec((1, LANES), lambda b, h: (0, 0)),
        ],
        out_specs=pl.BlockSpec((None, seq, LANES), lambda b, h: (h, b, 0)),
        out_shape=jax.ShapeDtypeStruct((HG_HEADS, batch * seq, LANES), BF16),
        scratch_shapes=[
            pltpu.VMEM((seq, LANES), F32),
            pltpu.VMEM((seq, LANES), BF16),
            pltpu.VMEM((seq, LANES), BF16),
            pltpu.VMEM((nc, HG_DIM, HG_DIM), F32),
            pltpu.VMEM((nc, HG_DIM, HG_DIM), F32),
            pltpu.VMEM((nc, SUBLANES, LANES), F32),
            pltpu.VMEM((nc, SUBLANES, LANES), F32),
            pltpu.VMEM((nc, HG_DIM, HG_DIM), BF16),
            pltpu.VMEM((nc, HG_DIM, HG_DIM), BF16),
        ],
        compiler_params=_cparams("parallel", "parallel"),
        name="hgrn2",
    )(z3, z3, z3, zf3, zf3, lb, gnorm.reshape(1, LANES))


def _natten_kernel(q_ref, k_ref, v_ref, bias_ref, o_ref):
    rows = q_ref.shape[0] // GRID_W
    kh = min(NA_KH, rows)
    win = kh * GRID_W
    nt = (((1,), (1,)), ((), ()))
    lane = lax.broadcasted_iota(jnp.int32, (GRID_W, LANES), 1)
    first = lane < NA_DIM

    def group(gi, carry):
        scored = []
        for u in range(NA_GROUP):
            r = gi * NA_GROUP + u
            rs = jnp.clip(r - kh // 2, 0, rows - kh)
            q2 = q_ref[pl.ds(pl.multiple_of(r * GRID_W, GRID_W), GRID_W), :] * (NA_DIM ** -0.5)
            zero = jnp.zeros_like(q2)
            q_st = jnp.concatenate([jnp.where(first, q2, zero), jnp.where(first, zero, q2)], axis=0)
            kw = k_ref[pl.ds(pl.multiple_of(rs * GRID_W, GRID_W), win), :]
            scored.append((lax.dot_general(q_st, kw, nt, preferred_element_type=F32), r, rs))
        probs = []
        for s, r, rs in scored:
            bm = bias_ref[r - rs]
            s = jnp.where(bm > 0.5 * MASK_NEG, s + bm, MASK_NEG)
            m = jnp.max(s, axis=-1, keepdims=True)
            p = jnp.exp(s - m)
            probs.append((p.astype(BF16), jnp.sum(p, axis=-1, keepdims=True), r, rs))
        for p, l, r, rs in probs:
            vw = v_ref[pl.ds(pl.multiple_of(rs * GRID_W, GRID_W), win), :]
            o_st = jnp.dot(p, vw, preferred_element_type=F32) / l
            o = jnp.where(first, o_st[:GRID_W], o_st[GRID_W:])
            o_ref[pl.ds(pl.multiple_of(r * GRID_W, GRID_W), GRID_W), :] = o.astype(o_ref.dtype)
        return carry

    lax.fori_loop(0, rows // NA_GROUP, group, 0)


def _natten_bias_table(rpb, rows):
    kh = min(NA_KH, rows)
    r = np.arange(rows)
    rs = np.clip(r - kh // 2, 0, rows - kh)
    n_case = int((r - rs).max()) + 1
    case = np.arange(n_case)
    dr = np.arange(kh)[None, :] - case[:, None] + (NA_KH - 1)
    c = np.arange(GRID_W)
    col_start = np.clip(c - NA_KW // 2, 0, GRID_W - NA_KW)
    col_mask = (c[None, :] >= col_start[:, None]) & (c[None, :] < col_start[:, None] + NA_KW)
    dc = np.clip(c[None, :] - c[:, None], -(NA_KW - 1), NA_KW - 1) + (NA_KW - 1)
    by_row = jnp.take(rpb.astype(F32), dr.reshape(-1), axis=1)
    by_row = by_row.reshape(rpb.shape[0], n_case, 1, kh, 1, 2 * NA_KW - 1)
    shape = (rpb.shape[0], n_case, GRID_W, kh, GRID_W)
    tab = jnp.full(shape, MASK_NEG, F32)
    for d in range(2 * NA_KW - 1):
        hit = ((dc == d) & col_mask)[None, None, :, None, :]
        tab = jnp.where(hit, by_row[..., d], tab)
    tab = tab.reshape(rpb.shape[0] // 2, 2, n_case, GRID_W, kh * GRID_W)
    return tab.transpose(0, 2, 1, 3, 4).reshape(rpb.shape[0] // 2, n_case, 2 * GRID_W, kh * GRID_W)


def _natten(z3, bias_tab, batch, seq):
    n_case, win = bias_tab.shape[1], bias_tab.shape[3]
    blk = lambda cb0: pl.BlockSpec((None, seq, LANES), lambda b, p: (cb0 + p, b, 0))
    return pl.pallas_call(
        _natten_kernel,
        grid=(batch, NA_HEADS // 2),
        in_specs=[
            blk(CB_NQ), blk(CB_NK), blk(CB_NV),
            pl.BlockSpec((None, n_case, 2 * GRID_W, win), lambda b, p: (p, 0, 0, 0)),
        ],
        out_specs=pl.BlockSpec((None, seq, LANES), lambda b, p: (p, b, 0)),
        out_shape=jax.ShapeDtypeStruct((NA_HEADS // 2, batch * seq, LANES), BF16),
        compiler_params=_cparams("parallel", "parallel"),
        name="natten",
    )(z3, z3, z3, bias_tab)


def _memattn_kernel(q_ref, kv_ref, o_ref):
    nt = (((1,), (1,)), ((), ()))
    for h in range(CA_HEADS):
        s = lax.dot_general(q_ref[h], kv_ref[h], nt, preferred_element_type=F32) * (CA_DIM ** -0.5)
        m = jnp.max(s, axis=-1, keepdims=True)
        p = jnp.exp(s - m)
        l = jnp.sum(p, axis=-1, keepdims=True)
        o = jnp.dot(p.astype(BF16), kv_ref[CA_HEADS + h], preferred_element_type=F32) / l
        o_ref[h] = o.astype(o_ref.dtype)


def _memattn(z3, kv3, batch, seq, mem_len, tq):
    nq = seq // tq
    return pl.pallas_call(
        _memattn_kernel,
        grid=(batch, nq),
        in_specs=[
            pl.BlockSpec((CA_HEADS, tq, LANES), lambda b, i: (CB_CQ // CA_HEADS, b * nq + i, 0)),
            pl.BlockSpec((2 * CA_HEADS, mem_len, LANES), lambda b, i: (0, b, 0)),
        ],
        out_specs=pl.BlockSpec((CA_HEADS, tq, LANES), lambda b, i: (0, b * nq + i, 0)),
        out_shape=jax.ShapeDtypeStruct((CA_HEADS, batch * seq, LANES), BF16),
        compiler_params=_cparams("parallel", "parallel"),
        name="memattn",
    )(z3, kv3)


def _cat(ref):
    return jnp.concatenate([ref[j] for j in range(ref.shape[0])], axis=-1)


def _merge_kernel(x_ref, ohg_ref, ona_ref, oca_ref, ghg_ref, gna_ref, gca_ref,
                  whg_ref, wna_ref, wca_ref, wout_ref, o_ref):
    def branch(o3_ref, w_ref, g_ref):
        y = jnp.dot(_cat(o3_ref), w_ref[...], preferred_element_type=F32)
        return _sigmoid(_cat(g_ref).astype(F32)) * y

    merged = branch(ohg_ref, whg_ref, ghg_ref)
    merged = merged + branch(ona_ref, wna_ref, gna_ref)
    merged = merged + branch(oca_ref, wca_ref, gca_ref)
    o_ref[...] = x_ref[...] + jnp.dot(merged.astype(BF16), wout_ref[...],
                                      preferred_element_type=F32)


def _merge(x2, ohg3, ona3, oca3, z3, w_hg_o, w_na_o, w_ca_o, w_out, tm):
    m, d = x2.shape
    ncb = d // LANES
    act = lambda n: pl.BlockSpec((n, tm, LANES), lambda i: (0, i, 0))
    gate = lambda cb0: pl.BlockSpec((ncb, tm, LANES), lambda i: (cb0 // ncb, i, 0))
    full = lambda a: pl.BlockSpec(a.shape, lambda i: (0, 0))
    return pl.pallas_call(
        _merge_kernel,
        grid=(m // tm,),
        in_specs=[
            pl.BlockSpec((tm, d), lambda i: (i, 0)),
            act(ohg3.shape[0]), act(ona3.shape[0]), act(oca3.shape[0]),
            gate(CB_GHG), gate(CB_GNA), gate(CB_GCA),
            full(w_hg_o), full(w_na_o), full(w_ca_o), full(w_out),
        ],
        out_specs=pl.BlockSpec((tm, d), lambda i: (i, 0)),
        out_shape=jax.ShapeDtypeStruct((m, d), F32),
        input_output_aliases={0: 0},
        compiler_params=_cparams("parallel"),
        name="merge",
    )(x2, ohg3, ona3, oca3, z3, z3, z3, w_hg_o, w_na_o, w_ca_o, w_out)


def _ffn_kernel(x_ref, xp_ref, xn_ref, gain_ref, wup_ref, cw_ref, cb_ref, wd_ref, fin_ref, o_ref,
                h_ref, y_ref, *, tiles_per_seq, final_norm, ts):
    i = pl.program_id(0)
    tm = x_ref.shape[0]
    halo = SUBLANES

    def normed(x):
        ms = jnp.mean(x * x, axis=-1, keepdims=True)
        return x * lax.rsqrt(ms + EPS) * gain_ref[...]

    keep_prev = (i % tiles_per_seq != 0).astype(F32)
    keep_next = (i % tiles_per_seq != tiles_per_seq - 1).astype(F32)
    h_ref[0:halo, :] = (normed(xp_ref[...]) * keep_prev).astype(BF16)
    h_ref[halo:halo + tm, :] = normed(x_ref[...]).astype(BF16)
    h_ref[halo + tm:, :] = (normed(xn_ref[...]) * keep_next).astype(BF16)

    def conv_proj(cols):
        u = jnp.dot(h_ref[...], wup_ref[:, cols], preferred_element_type=F32)
        out = u[halo - 1:halo - 1 + tm] * cw_ref[0:1, cols]
        out = out + u[halo:halo + tm] * cw_ref[1:2, cols]
        out = out + u[halo + 1:halo + 1 + tm] * cw_ref[2:3, cols]
        return out + cb_ref[:, cols]

    def sub_tile(s, carry):
        cols_a = pl.ds(pl.multiple_of(s * ts, ts), ts)
        cols_g = pl.ds(pl.multiple_of(D_FF + s * ts, ts), ts)
        a = conv_proj(cols_a)
        g = conv_proj(cols_g)
        cdf = 0.5 * (1.0 + jnp.tanh(np.float32(np.sqrt(2 / np.pi)) * (a + 0.044715 * (a * a * a))))
        y_ref[:, cols_a] = (a * cdf * g).astype(BF16)
        return carry

    lax.fori_loop(0, D_FF // ts, sub_tile, 0, unroll=True)

    out = x_ref[...] + jnp.dot(y_ref[...], wd_ref[...], preferred_element_type=F32)
    if final_norm:
        ms = jnp.mean(out * out, axis=-1, keepdims=True)
        out = out * lax.rsqrt(ms + EPS) * fin_ref[...]
    o_ref[...] = out


def _conv_ffn(x2, gain, w_up, conv_w, conv_b, w_down, fin_gain, seq, tm, ts, final_norm):
    m, d = x2.shape
    hb = tm // SUBLANES
    n_halo = m // SUBLANES
    kern = functools.partial(_ffn_kernel, tiles_per_seq=seq // tm, final_norm=final_norm, ts=ts)
    resident = lambda a: pl.BlockSpec(a.shape, lambda i: (0, 0), pipeline_mode=pl.Buffered(1))
    conv_b = conv_b.reshape(1, -1)
    return pl.pallas_call(
        kern,
        grid=(m // tm,),
        in_specs=[
            pl.BlockSpec((tm, d), lambda i: (i, 0)),
            pl.BlockSpec((SUBLANES, d), lambda i: (jnp.maximum(i * hb - 1, 0), 0)),
            pl.BlockSpec((SUBLANES, d), lambda i: (jnp.minimum((i + 1) * hb, n_halo - 1), 0)),
            pl.BlockSpec((1, d), lambda i: (0, 0)),
            resident(w_up), resident(conv_w), resident(conv_b), resident(w_down),
            pl.BlockSpec((1, d), lambda i: (0, 0)),
        ],
        out_specs=pl.BlockSpec((tm, d), lambda i: (i, 0)),
        out_shape=jax.ShapeDtypeStruct((m, d), F32),
        scratch_shapes=[
            pltpu.VMEM((tm + 2 * SUBLANES, d), BF16),
            pltpu.VMEM((tm, D_FF), BF16),
        ],
        compiler_params=_cparams("parallel"),
        name="conv_ffn",
    )(x2, x2, x2, gain.reshape(1, d), w_up, conv_w, conv_b, w_down, fin_gain.reshape(1, d))


def kernel(x, mem, norm_mix, w_in, hg_lb_logits, hg_gnorm, w_hg_o, na_rpb, w_na_o, mem_norm,
           w_mem_kv, w_ca_o, w_out, norm_ffn, w_up, conv_w, conv_b, w_down, norm_final):
    batch, seq, d = x.shape
    mem_len = mem.shape[1]
    depth = w_in.shape[0]
    assert d == D_MODEL and seq % GRID_W == 0 and seq % HG_CHUNK == 0
    hgw = HG_HEADS * HG_DIM

    p_lb = jax.nn.softmax(hg_lb_logits.astype(F32), axis=0)
    lower_bounds = jnp.clip(jnp.cumsum(p_lb, axis=0) - p_lb[0], 0.0, 1.0)

    w_in_main = jnp.concatenate([w_in[:, :, :2 * hgw], w_in[:, :, 4 * hgw:]], axis=-1).astype(BF16)
    w_in_forget = w_in[:, :, 2 * hgw:4 * hgw].astype(BF16)
    assert w_in_main.shape[-1] == N_CB * LANES

    x2 = x.reshape(batch * seq, d)
    mem2 = mem.reshape(batch * mem_len, d)
    bf = lambda a: a.astype(BF16)

    for l in range(depth):
        z3 = _norm_matmul(x2, norm_mix[l], w_in_main[l], BF16, tm=1024, tn=2048)
        zf3 = _norm_matmul(x2, norm_mix[l], w_in_forget[l], F32, tm=1024, tn=1024)
        kv3 = _norm_matmul(mem2, mem_norm, bf(w_mem_kv[l]), BF16, tm=batch * mem_len, tn=1024)

        ohg3 = _hgrn2(z3, zf3, lower_bounds[l], hg_gnorm[l], batch, seq)
        ona3 = _natten(z3, _natten_bias_table(na_rpb[l], seq // GRID_W), batch, seq)
        oca3 = _memattn(z3, kv3, batch, seq, mem_len, tq=512)

        x2 = _merge(x2, ohg3, ona3, oca3, z3, bf(w_hg_o[l]), bf(w_na_o[l]), bf(w_ca_o[l]),
                    bf(w_out[l]), tm=512)
        x2 = _conv_ffn(x2, norm_ffn[l], bf(w_up[l]), conv_w[l], conv_b[l], bf(w_down[l]),
                       norm_final, seq, tm=512, ts=256, final_norm=(l == depth - 1))
    return x2.reshape(batch, seq, d)
```

```python
import functools

import numpy as np
import jax
import jax.numpy as jnp
from jax import lax
from jax.experimental import pallas as pl
from jax.experimental.pallas import tpu as pltpu

D_MODEL = 1024
GRID_W = 64
HG_HEADS = 8
HG_DIM = 128
NA_HEADS = 8
NA_DIM = 64
NA_KH = 8
NA_KW = 16
CA_HEADS = 4
CA_DIM = 128
D_FF = 2816
CONV_W = 3
EPS = 1e-6
F_FLOOR = 1e-12
MASK_NEG = -1e30

LANES = 128
SUBLANES = 8
VMEM_LIMIT = 48 * 1024 * 1024

HG_CHUNK = 128
NA_GROUP = 4

CB_HQ, CB_HI, CB_HG = 0, 8, 16
CB_NQ, CB_NK, CB_NV = 24, 28, 32
CB_CQ = 36
CB_GHG, CB_GNA, CB_GCA = 40, 48, 56
N_CB = 64

F32 = jnp.float32
BF16 = jnp.bfloat16


def _cparams(*sem):
    return pltpu.CompilerParams(dimension_semantics=sem, vmem_limit_bytes=VMEM_LIMIT)


def _sigmoid(x):
    return 1.0 / (1.0 + jnp.exp(-x))


ACT_NONE, ACT_SILU, ACT_SILU_HGQ = 0, 1, 2


def _normalise_rows(x_ref, g_ref, h_ref):
    @pl.when(pl.program_id(1) == 0)
    def _():
        x = x_ref[...]
        ms = jnp.mean(x * x, axis=-1, keepdims=True)
        h_ref[...] = (x * lax.rsqrt(ms + EPS) * g_ref[...]).astype(BF16)


def _norm_mm_kernel(x_ref, g_ref, w_ref, o_ref, h_ref, *, plans):
    _normalise_rows(x_ref, g_ref, h_ref)
    acc = jnp.dot(h_ref[...], w_ref[...], preferred_element_type=F32)

    def store(plan):
        for jj, act in enumerate(plan):
            z = acc[:, jj * LANES:(jj + 1) * LANES]
            if act == ACT_SILU:
                z = z * _sigmoid(z)
            elif act == ACT_SILU_HGQ:
                z = z * _sigmoid(z) * (HG_DIM ** -0.5)
            o_ref[jj] = z.astype(o_ref.dtype)

    distinct = sorted(set(plans))
    if len(distinct) == 1:
        store(distinct[0])
    else:
        j = pl.program_id(1)
        for plan in distinct:
            hit = functools.reduce(jnp.logical_or, [j == t for t, p in enumerate(plans) if p == plan])
            pl.when(hit)(functools.partial(store, plan))


def _norm_matmul(x2, gain, w_bf16, layer, col_tiles, plans, tm, tn):
    m, d = x2.shape
    first, skip_from, skip = col_tiles
    n_tiles = len(plans)
    return pl.pallas_call(
        functools.partial(_norm_mm_kernel, plans=plans),
        grid=(m // tm, n_tiles),
        in_specs=[
            pl.BlockSpec((tm, d), lambda i, j: (i, 0)),
            pl.BlockSpec((None, 1, d), lambda i, j: (layer, 0, 0)),
            pl.BlockSpec((None, d, tn),
                         lambda i, j: (layer, 0, first + j + jnp.where(j >= skip_from, skip, 0))),
        ],
        out_specs=pl.BlockSpec((tn // LANES, tm, LANES), lambda i, j: (j, i, 0)),
        out_shape=jax.ShapeDtypeStruct((n_tiles * tn // LANES, m, LANES), BF16),
        scratch_shapes=[pltpu.VMEM((tm, d), BF16)],
        compiler_params=_cparams("parallel", "arbitrary"),
        name="norm_matmul",
    )(x2, gain, w_bf16)


def _gate_parts(z, lb):
    t = jnp.exp(-jnp.abs(z))
    r = 1.0 / (1.0 + t)
    pos = z >= 0.0
    sig = jnp.where(pos, r, t * r)
    sig_neg = jnp.where(pos, t * r, r)
    f = jnp.maximum(lb + (1.0 - lb) * sig, F_FLOOR)
    k = (1.0 - lb) * sig_neg
    return f, k


def _norm_mm_gate_kernel(x_ref, g_ref, w_ref, lb_ref, f_ref, k_ref, h_ref):
    _normalise_rows(x_ref, g_ref, h_ref)
    acc = jnp.dot(h_ref[...], w_ref[...], preferred_element_type=F32)
    for jj in range(f_ref.shape[0]):
        cols = slice(jj * LANES, (jj + 1) * LANES)
        f, k = _gate_parts(acc[:, cols], lb_ref[:, cols])
        f_ref[jj] = f
        k_ref[jj] = k.astype(k_ref.dtype)


def _norm_matmul_gates(x2, gain, w_bf16, lb2, layer, first_tile, n_tiles, tm, tn):
    m, d = x2.shape
    ncb = n_tiles * tn // LANES
    out_spec = pl.BlockSpec((tn // LANES, tm, LANES), lambda i, j: (j, i, 0))
    return pl.pallas_call(
        _norm_mm_gate_kernel,
        grid=(m // tm, n_tiles),
        in_specs=[
            pl.BlockSpec((tm, d), lambda i, j: (i, 0)),
            pl.BlockSpec((None, 1, d), lambda i, j: (layer, 0, 0)),
            pl.BlockSpec((None, d, tn), lambda i, j: (layer, 0, first_tile + j)),
            pl.BlockSpec((None, 1, tn), lambda i, j: (layer, 0, j)),
        ],
        out_specs=[out_spec, out_spec],
        out_shape=[jax.ShapeDtypeStruct((ncb, m, LANES), F32),
                   jax.ShapeDtypeStruct((ncb, m, LANES), BF16)],
        scratch_shapes=[pltpu.VMEM((tm, d), BF16)],
        compiler_params=_cparams("parallel", "arbitrary"),
        name="norm_matmul_gates",
    )(x2, gain, w_bf16, lb2)


def _slabs(x):
    return [x[i:i + SUBLANES, :] for i in range(0, x.shape[0], SUBLANES)]


def _join(slabs):
    return jnp.concatenate(slabs, axis=0)


def _decayed_operands(q, k, f, row8, reverse):
    n = len(f)
    qe, kx, tot = [q[i] * f[i] for i in range(n)], list(k), list(f)
    levels = {1: (qe, kx)}
    b = 1
    while b < SUBLANES:
        later = (row8 & b) != 0
        if reverse:
            later = jnp.logical_not(later)
        back, fwd = (SUBLANES - b, b) if reverse else (b, SUBLANES - b)
        prev = [pltpu.roll(t, back, 0) for t in tot]
        nxt = prev if 2 * b == SUBLANES else [pltpu.roll(t, fwd, 0) for t in tot]
        qe = [qe[i] * jnp.where(later, prev[i], 1.0) for i in range(n)]
        kx = [kx[i] * jnp.where(later, 1.0, nxt[i]) for i in range(n)]
        if 2 * b == SUBLANES:
            tot = [tot[i] * prev[i] for i in range(n)]
        else:
            tot = [tot[i] * jnp.where(later, prev[i], nxt[i]) for i in range(n)]
        b *= 2
        levels[b] = (qe, kx)
    m = 1
    while m < n:
        qe, kx, nt = list(qe), list(kx), []
        for p in range(0, n, 2 * m):
            first, second = (p + m, p) if reverse else (p, p + m)
            t_first, t_second = tot[first // m], tot[second // m]
            for i in range(m):
                qe[second + i] = qe[second + i] * t_first
                kx[first + i] = kx[first + i] * t_second
            nt.append(t_first * t_second)
        tot = nt
        m *= 2
        levels[m * SUBLANES] = (qe, kx)
    return levels, tot[0]


def _hgrn2_kernel(q_ref, v_ref, g_ref, ff_ref, fb_ref, kf_ref, kb_ref, gn_ref, o_ref,
                  oi_ref, qf_ref, qb_ref, kvf_ref, kvb_ref, df_ref, db_ref, sf_ref, sb_ref):
    c = HG_CHUNK
    nc = q_ref.shape[0] // c
    nt = (((1,), (1,)), ((), ()))
    tn = (((0,), (0,)), ((), ()))
    row8 = lax.broadcasted_iota(jnp.int32, (SUBLANES, LANES), 0)
    pair_xor = (lax.broadcasted_iota(jnp.int32, (c, c), 0)
                ^ lax.broadcasted_iota(jnp.int32, (c, c), 1))
    n_slab = c // SUBLANES

    def intra(ci, carry):
        rows = pl.ds(pl.multiple_of(ci * c, c), c)
        q = _slabs(q_ref[rows, :].astype(F32))
        v = v_ref[rows, :]
        k_f = _slabs(kf_ref[rows, :].astype(F32))
        k_b = _slabs(kb_ref[rows, :].astype(F32))
        lev_f, tot_f = _decayed_operands(q, k_f, _slabs(ff_ref[rows, :]), row8, False)
        lev_b, tot_b = _decayed_operands(q, k_b, _slabs(fb_ref[rows, :]), row8, True)

        scores = None
        b = c // 2
        while b >= 1:
            (q_f, x_f), (q_b, x_b) = lev_f[b], lev_b[b]
            if b >= SUBLANES:
                is_right = [(i // (b // SUBLANES)) % 2 == 1 for i in range(n_slab)]
                qh = [q_f[i] if is_right[i] else q_b[i] for i in range(n_slab)]
                kh = [x_b[i] if is_right[i] else x_f[i] for i in range(n_slab)]
            else:
                right = (row8 & b) != 0
                qh = [jnp.where(right, q_f[i], q_b[i]) for i in range(n_slab)]
                kh = [jnp.where(right, x_b[i], x_f[i]) for i in range(n_slab)]
            r = lax.dot_general(_join(qh).astype(BF16), _join(kh).astype(BF16), nt,
                                preferred_element_type=F32)
            scores = r if scores is None else jnp.where(pair_xor < 2 * b, r, scores)
            b //= 2
        k_both = [k_f[i] + k_b[i] for i in range(n_slab)]
        r = lax.dot_general(q_ref[rows, :], _join(k_both).astype(BF16), nt,
                            preferred_element_type=F32)
        scores = jnp.where(pair_xor < 1, r, scores)
        oi_ref[rows, :] = jnp.dot(scores.astype(BF16), v, preferred_element_type=F32)

        (q_f, x_f), (q_b, x_b) = lev_f[c], lev_b[c]
        qf_ref[rows, :] = _join(q_f).astype(BF16)
        qb_ref[rows, :] = _join(q_b).astype(BF16)
        kvf_ref[ci] = lax.dot_general(v, _join(x_f).astype(BF16), tn, preferred_element_type=F32)
        kvb_ref[ci] = lax.dot_general(v, _join(x_b).astype(BF16), tn, preferred_element_type=F32)
        df_ref[ci] = tot_f
        db_ref[ci] = tot_b
        return carry

    lax.fori_loop(0, nc, intra, 0, unroll=2)

    def scan_states(i, carry):
        s_f, s_b = carry
        cb = nc - 1 - i
        sf_ref[i] = s_f.astype(BF16)
        sb_ref[cb] = s_b.astype(BF16)
        s_f = s_f * df_ref[i][0:1, :] + kvf_ref[i]
        s_b = s_b * db_ref[cb][0:1, :] + kvb_ref[cb]
        return s_f, s_b

    zero = jnp.zeros((HG_DIM, HG_DIM), F32)
    lax.fori_loop(0, nc, scan_states, (zero, zero))

    gn = gn_ref[...]

    def finish(ci, carry):
        rows = pl.ds(pl.multiple_of(ci * c, c), c)
        o = oi_ref[rows, :]
        o = o + lax.dot_general(qf_ref[rows, :], sf_ref[ci], nt, preferred_element_type=F32)
        o = o + lax.dot_general(qb_ref[rows, :], sb_ref[ci], nt, preferred_element_type=F32)
        ms = jnp.mean(o * o, axis=-1, keepdims=True)
        o = o * lax.rsqrt(ms + EPS) * gn
        o_ref[rows, :] = (o * g_ref[rows, :].astype(F32)).astype(o_ref.dtype)
        return carry

    lax.fori_loop(0, nc, finish, 0, unroll=4)


def _hgrn2(z3, f3, k3, gnorm3, layer, batch, seq):
    c = HG_CHUNK
    nc = seq // c
    blk = lambda cb0: pl.BlockSpec((None, seq, LANES), lambda b, h: (cb0 + h, b, 0))
    return pl.pallas_call(
        _hgrn2_kernel,
        grid=(batch, HG_HEADS),
        in_specs=[
            blk(CB_HQ), blk(CB_HI), blk(CB_HG),
            blk(0), blk(HG_HEADS), blk(0), blk(HG_HEADS),
            pl.BlockSpec((None, 1, LANES), lambda b, h: (layer, 0, 0)),
        ],
        out_specs=pl.BlockSpec((None, seq, LANES), lambda b, h: (h, b, 0)),
        out_shape=jax.ShapeDtypeStruct((HG_HEADS, batch * seq, LANES), BF16),
        scratch_shapes=[
            pltpu.VMEM((seq, LANES), F32),
            pltpu.VMEM((seq, LANES), BF16),
            pltpu.VMEM((seq, LANES), BF16),
            pltpu.VMEM((nc, HG_DIM, HG_DIM), F32),
            pltpu.VMEM((nc, HG_DIM, HG_DIM), F32),
            pltpu.VMEM((nc, SUBLANES, LANES), F32),
            pltpu.VMEM((nc, SUBLANES, LANES), F32),
            pltpu.VMEM((nc, HG_DIM, HG_DIM), BF16),
            pltpu.VMEM((nc, HG_DIM, HG_DIM), BF16),
        ],
        compiler_params=_cparams("parallel", "parallel"),
        name="hgrn2",
    )(z3, z3, z3, f3, f3, k3, k3, gnorm3)


def _natten_kernel(q_ref, k_ref, v_ref, bias_ref, o_ref):
    rows = q_ref.shape[0] // GRID_W
    kh = min(NA_KH, rows)
    win = kh * GRID_W
    nt = (((1,), (1,)), ((), ()))
    lane = lax.broadcasted_iota(jnp.int32, (GRID_W, LANES), 1)
    first = lane < NA_DIM

    def group(gi, carry):
        scored = []
        for u in range(NA_GROUP):
            r = gi * NA_GROUP + u
            rs = jnp.clip(r - kh // 2, 0, rows - kh)
            q2 = q_ref[pl.ds(pl.multiple_of(r * GRID_W, GRID_W), GRID_W), :] * (NA_DIM ** -0.5)
            zero = jnp.zeros_like(q2)
            q_st = jnp.concatenate([jnp.where(first, q2, zero), jnp.where(first, zero, q2)], axis=0)
            kw = k_ref[pl.ds(pl.multiple_of(rs * GRID_W, GRID_W), win), :]
            scored.append((lax.dot_general(q_st, kw, nt, preferred_element_type=F32), r, rs))
        probs = []
        for s, r, rs in scored:
            bm = bias_ref[r - rs]
            s = jnp.where(bm > 0.5 * MASK_NEG, s + bm, MASK_NEG)
            m = jnp.max(s, axis=-1, keepdims=True)
            p = jnp.exp(s - m)
            probs.append((p.astype(BF16), jnp.sum(p, axis=-1, keepdims=True), r, rs))
        for p, l, r, rs in probs:
            vw = v_ref[pl.ds(pl.multiple_of(rs * GRID_W, GRID_W), win), :]
            o_st = jnp.dot(p, vw, preferred_element_type=F32) / l
            o = jnp.where(first, o_st[:GRID_W], o_st[GRID_W:])
            o_ref[pl.ds(pl.multiple_of(r * GRID_W, GRID_W), GRID_W), :] = o.astype(o_ref.dtype)
        return carry

    lax.fori_loop(0, rows // NA_GROUP, group, 0)


def _natten_bias_table(rpb, rows):
    kh = min(NA_KH, rows)
    r = np.arange(rows)
    rs = np.clip(r - kh // 2, 0, rows - kh)
    n_case = int((r - rs).max()) + 1
    case = np.arange(n_case)
    dr = np.arange(kh)[None, :] - case[:, None] + (NA_KH - 1)
    c = np.arange(GRID_W)
    col_start = np.clip(c - NA_KW // 2, 0, GRID_W - NA_KW)
    col_mask = (c[None, :] >= col_start[:, None]) & (c[None, :] < col_start[:, None] + NA_KW)
    dc = np.clip(c[None, :] - c[:, None], -(NA_KW - 1), NA_KW - 1) + (NA_KW - 1)
    depth, heads, _, n_dc = rpb.shape
    by_row = jnp.take(rpb.astype(F32), dr.reshape(-1), axis=2)
    by_row = by_row.reshape(depth, heads // 2, 2, n_case, kh, n_dc).transpose(0, 1, 3, 2, 4, 5)
    by_row = by_row[:, :, :, :, None, :, None, :]
    tab = jnp.full((depth, heads // 2, n_case, 2, GRID_W, kh, GRID_W), MASK_NEG, F32)
    for d in range(n_dc):
        hit = ((dc == d) & col_mask)[None, None, None, None, :, None, :]
        tab = jnp.where(hit, by_row[..., d], tab)
    return tab.reshape(depth, heads // 2, n_case, 2 * GRID_W, kh * GRID_W)


def _natten(z3, bias_tab, layer, batch, seq):
    n_case, win = bias_tab.shape[2], bias_tab.shape[4]
    blk = lambda cb0: pl.BlockSpec((None, seq, LANES), lambda b, p: (cb0 + p, b, 0))
    return pl.pallas_call(
        _natten_kernel,
        grid=(batch, NA_HEADS // 2),
        in_specs=[
            blk(CB_NQ), blk(CB_NK), blk(CB_NV),
            pl.BlockSpec((None, None, n_case, 2 * GRID_W, win), lambda b, p: (layer, p, 0, 0, 0)),
        ],
        out_specs=pl.BlockSpec((None, seq, LANES), lambda b, p: (p, b, 0)),
        out_shape=jax.ShapeDtypeStruct((NA_HEADS // 2, batch * seq, LANES), BF16),
        compiler_params=_cparams("parallel", "parallel"),
        name="natten",
    )(z3, z3, z3, bias_tab)


def _memattn_kernel(q_ref, kv_ref, o_ref):
    nt = (((1,), (1,)), ((), ()))
    for h in range(CA_HEADS):
        s = lax.dot_general(q_ref[h], kv_ref[h], nt, preferred_element_type=F32) * (CA_DIM ** -0.5)
        m = jnp.max(s, axis=-1, keepdims=True)
        p = jnp.exp(s - m)
        l = jnp.sum(p, axis=-1, keepdims=True)
        o = jnp.dot(p.astype(BF16), kv_ref[CA_HEADS + h], preferred_element_type=F32) / l
        o_ref[h] = o.astype(o_ref.dtype)


def _memattn(z3, kv3, batch, seq, mem_len, tq):
    nq = seq // tq
    return pl.pallas_call(
        _memattn_kernel,
        grid=(batch, nq),
        in_specs=[
            pl.BlockSpec((CA_HEADS, tq, LANES), lambda b, i: (CB_CQ // CA_HEADS, b * nq + i, 0)),
            pl.BlockSpec((2 * CA_HEADS, mem_len, LANES), lambda b, i: (0, b, 0)),
        ],
        out_specs=pl.BlockSpec((CA_HEADS, tq, LANES), lambda b, i: (0, b * nq + i, 0)),
        out_shape=jax.ShapeDtypeStruct((CA_HEADS, batch * seq, LANES), BF16),
        compiler_params=_cparams("parallel", "parallel"),
        name="memattn",
    )(z3, kv3)


def _cat(ref):
    return jnp.concatenate([ref[j] for j in range(ref.shape[0])], axis=-1)


def _merge_kernel(x_ref, ohg_ref, ona_ref, oca_ref, ghg_ref, gna_ref, gca_ref,
                  whg_ref, wna_ref, wca_ref, wout_ref, o_ref):
    def branch(o3_ref, w_ref, g_ref):
        y = jnp.dot(_cat(o3_ref), w_ref[...], preferred_element_type=F32)
        return _sigmoid(_cat(g_ref).astype(F32)) * y

    merged = branch(ohg_ref, whg_ref, ghg_ref)
    merged = merged + branch(ona_ref, wna_ref, gna_ref)
    merged = merged + branch(oca_ref, wca_ref, gca_ref)
    o_ref[...] = x_ref[...] + jnp.dot(merged.astype(BF16), wout_ref[...],
                                      preferred_element_type=F32)


def _merge(x2, ohg3, ona3, oca3, z3, w_hg_o, w_na_o, w_ca_o, w_out, layer, tm):
    m, d = x2.shape
    ncb = d // LANES
    act = lambda n: pl.BlockSpec((n, tm, LANES), lambda i: (0, i, 0))
    gate = lambda cb0: pl.BlockSpec((ncb, tm, LANES), lambda i: (cb0 // ncb, i, 0))
    full = lambda a: pl.BlockSpec((None,) + a.shape[1:], lambda i: (layer, 0, 0))
    return pl.pallas_call(
        _merge_kernel,
        grid=(m // tm,),
        in_specs=[
            pl.BlockSpec((tm, d), lambda i: (i, 0)),
            act(ohg3.shape[0]), act(ona3.shape[0]), act(oca3.shape[0]),
            gate(CB_GHG), gate(CB_GNA), gate(CB_GCA),
            full(w_hg_o), full(w_na_o), full(w_ca_o), full(w_out),
        ],
        out_specs=pl.BlockSpec((tm, d), lambda i: (i, 0)),
        out_shape=jax.ShapeDtypeStruct((m, d), F32),
        compiler_params=_cparams("parallel"),
        name="merge",
    )(x2, ohg3, ona3, oca3, z3, z3, z3, w_hg_o, w_na_o, w_ca_o, w_out)


def _ffn_kernel(x_ref, xp_ref, xn_ref, gain_ref, wup_ref, cw_ref, cb_ref, wd_ref, fin_ref, o_ref,
                h_ref, y_ref, *, tiles_per_seq, final_norm, ts):
    i = pl.program_id(0)
    tm = x_ref.shape[0]
    halo = SUBLANES

    def normed(x):
        ms = jnp.mean(x * x, axis=-1, keepdims=True)
        return x * lax.rsqrt(ms + EPS) * gain_ref[...]

    keep_prev = (i % tiles_per_seq != 0).astype(F32)
    keep_next = (i % tiles_per_seq != tiles_per_seq - 1).astype(F32)
    h_ref[0:halo, :] = (normed(xp_ref[...]) * keep_prev).astype(BF16)
    h_ref[halo:halo + tm, :] = normed(x_ref[...]).astype(BF16)
    h_ref[halo + tm:, :] = (normed(xn_ref[...]) * keep_next).astype(BF16)

    def conv_proj(cols):
        u = jnp.dot(h_ref[...], wup_ref[:, cols], preferred_element_type=F32)
        out = u[halo - 1:halo - 1 + tm] * cw_ref[0:1, cols]
        out = out + u[halo:halo + tm] * cw_ref[1:2, cols]
        out = out + u[halo + 1:halo + 1 + tm] * cw_ref[2:3, cols]
        return out + cb_ref[:, cols]

    def sub_tile(s, carry):
        cols_a = pl.ds(pl.multiple_of(s * ts, ts), ts)
        cols_g = pl.ds(pl.multiple_of(D_FF + s * ts, ts), ts)
        a = conv_proj(cols_a)
        g = conv_proj(cols_g)
        cdf = 0.5 * (1.0 + jnp.tanh(np.float32(np.sqrt(2 / np.pi)) * (a + 0.044715 * (a * a * a))))
        y_ref[:, cols_a] = (a * cdf * g).astype(BF16)
        return carry

    lax.fori_loop(0, D_FF // ts, sub_tile, 0, unroll=True)

    out = x_ref[...] + jnp.dot(y_ref[...], wd_ref[...], preferred_element_type=F32)
    if final_norm:
        ms = jnp.mean(out * out, axis=-1, keepdims=True)
        out = out * lax.rsqrt(ms + EPS) * fin_ref[...]
    o_ref[...] = out


def _conv_ffn(x2, gain3, w_up, conv_w, conv_b3, w_down, fin_gain, layer, seq, tm, ts, final_norm):
    m, d = x2.shape
    hb = tm // SUBLANES
    n_halo = m // SUBLANES
    kern = functools.partial(_ffn_kernel, tiles_per_seq=seq // tm, final_norm=final_norm, ts=ts)
    resident = lambda a: pl.BlockSpec((None,) + a.shape[1:], lambda i: (layer, 0, 0),
                                      pipeline_mode=pl.Buffered(1))
    return pl.pallas_call(
        kern,
        grid=(m // tm,),
        in_specs=[
            pl.BlockSpec((tm, d), lambda i: (i, 0)),
            pl.BlockSpec((SUBLANES, d), lambda i: (jnp.maximum(i * hb - 1, 0), 0)),
            pl.BlockSpec((SUBLANES, d), lambda i: (jnp.minimum((i + 1) * hb, n_halo - 1), 0)),
            pl.BlockSpec((None, 1, d), lambda i: (layer, 0, 0)),
            resident(w_up), resident(conv_w), resident(conv_b3), resident(w_down),
            pl.BlockSpec((1, d), lambda i: (0, 0)),
        ],
        out_specs=pl.BlockSpec((tm, d), lambda i: (i, 0)),
        out_shape=jax.ShapeDtypeStruct((m, d), F32),
        scratch_shapes=[
            pltpu.VMEM((tm + 2 * SUBLANES, d), BF16),
            pltpu.VMEM((tm, D_FF), BF16),
        ],
        compiler_params=_cparams("parallel"),
        name="conv_ffn",
    )(x2, x2, x2, gain3, w_up, conv_w, conv_b3, w_down, fin_gain.reshape(1, d))


def kernel(x, mem, norm_mix, w_in, hg_lb_logits, hg_gnorm, w_hg_o, na_rpb, w_na_o, mem_norm,
           w_mem_kv, w_ca_o, w_out, norm_ffn, w_up, conv_w, conv_b, w_down, norm_final):
    batch, seq, d = x.shape
    mem_len = mem.shape[1]
    depth = w_in.shape[0]
    assert d == D_MODEL and seq % GRID_W == 0 and seq % HG_CHUNK == 0
    hgw = HG_HEADS * HG_DIM

    p_lb = jax.nn.softmax(hg_lb_logits.astype(F32), axis=0)
    lower_bounds = jnp.clip(jnp.cumsum(p_lb, axis=0) - p_lb[0], 0.0, 1.0)

    bf = lambda a: a.astype(BF16)
    w_in, w_mem_kv, w_hg_o, w_na_o, w_ca_o, w_out, w_up, w_down = map(
        bf, (w_in, w_mem_kv, w_hg_o, w_na_o, w_ca_o, w_out, w_up, w_down))
    vec3 = lambda a: a.reshape(depth, 1, -1)
    norm_mix3, norm_ffn3, gnorm3, conv_b3 = map(vec3, (norm_mix, norm_ffn, hg_gnorm, conv_b))
    lb3 = vec3(lower_bounds)
    mem_norm3 = jnp.broadcast_to(mem_norm.reshape(1, 1, d), (depth, 1, d))
    bias_tab = _natten_bias_table(na_rpb, seq // GRID_W)

    tn = 2 * hgw
    blocks = hgw // LANES
    plans = ((ACT_SILU_HGQ,) * blocks + (ACT_NONE,) * blocks,
             (ACT_SILU,) * blocks + (ACT_NONE,) * blocks,
             (ACT_NONE,) * (2 * blocks), (ACT_NONE,) * (2 * blocks))
    assert len(plans) * tn == N_CB * LANES and w_in.shape[-1] == (len(plans) + 1) * tn
    kv_plans = ((ACT_NONE,) * (w_mem_kv.shape[-1] // LANES),)

    x2 = x.reshape(batch * seq, d)
    mem2 = mem.reshape(batch * mem_len, d)

    for l in range(depth):
        z3 = _norm_matmul(x2, norm_mix3, w_in, l, (0, 1, 1), plans, tm=1024, tn=tn)
        f3, k3 = _norm_matmul_gates(x2, norm_mix3, w_in, lb3, l, first_tile=2, n_tiles=2,
                                    tm=1024, tn=hgw)
        kv3 = _norm_matmul(mem2, mem_norm3, w_mem_kv, l, (0, 1, 0), kv_plans,
                           tm=batch * mem_len, tn=w_mem_kv.shape[-1])

        ohg3 = _hgrn2(z3, f3, k3, gnorm3, l, batch, seq)
        ona3 = _natten(z3, bias_tab, l, batch, seq)
        oca3 = _memattn(z3, kv3, batch, seq, mem_len, tq=512)

        x2 = _merge(x2, ohg3, ona3, oca3, z3, w_hg_o, w_na_o, w_ca_o, w_out, l, tm=512)
        x2 = _conv_ffn(x2, norm_ffn3, w_up, conv_w, conv_b3, w_down, norm_final, l, seq,
                       tm=512, ts=256, final_norm=(l == depth - 1))
    return x2.reshape(batch, seq, d)
```

```python
import functools

import numpy as np
import jax
import jax.numpy as jnp
from jax import lax
from jax.experimental import pallas as pl
from jax.experimental.pallas import tpu as pltpu

D_MODEL = 1024
GRID_W = 64
HG_HEADS = 8
HG_DIM = 128
NA_HEADS = 8
NA_DIM = 64
NA_KH = 8
NA_KW = 16
CA_HEADS = 4
CA_DIM = 128
D_FF = 2816
CONV_W = 3
EPS = 1e-6
F_FLOOR = 1e-12
MASK_NEG = -1e30

LANES = 128
SUBLANES = 8
VMEM_LIMIT = 56 * 1024 * 1024

HG_CHUNK = 128
NA_GROUP = 4

CB_HQ, CB_HI, CB_HG = 0, 8, 16
CB_NQ, CB_NK, CB_NV = 24, 28, 32
CB_CQ = 36
CB_GHG, CB_GNA, CB_GCA = 40, 48, 56
N_CB = 64

F32 = jnp.float32
BF16 = jnp.bfloat16


def _cparams(*sem):
    return pltpu.CompilerParams(dimension_semantics=sem, vmem_limit_bytes=VMEM_LIMIT)


def _sigmoid(x):
    return 1.0 / (1.0 + jnp.exp(-x))


PROJ_SUB = 512


def _normalise_rows(x_ref, g_ref, h_ref):
    @pl.when(pl.program_id(1) == 0)
    def _():
        x = x_ref[...]
        ms = jnp.mean(x * x, axis=-1, keepdims=True)
        h_ref[...] = (x * lax.rsqrt(ms + EPS) * g_ref[...]).astype(BF16)


def _sub_tiles(h_ref, w_ref):
    for s in range(w_ref.shape[1] // PROJ_SUB):
        acc = jnp.dot(h_ref[...], w_ref[:, s * PROJ_SUB:(s + 1) * PROJ_SUB],
                      preferred_element_type=F32)
        yield s * (PROJ_SUB // LANES), acc


def _norm_mm_kernel(x_ref, g_ref, w_ref, o_ref, h_ref):
    _normalise_rows(x_ref, g_ref, h_ref)
    for cb0, acc in _sub_tiles(h_ref, w_ref):
        for jj in range(PROJ_SUB // LANES):
            o_ref[cb0 + jj] = acc[:, jj * LANES:(jj + 1) * LANES].astype(o_ref.dtype)


def _norm_matmul(x2, gain3, w_bf16, layer, tm, tn):
    m, d = x2.shape
    n = w_bf16.shape[-1]
    return pl.pallas_call(
        _norm_mm_kernel,
        grid=(m // tm, n // tn),
        in_specs=[
            pl.BlockSpec((tm, d), lambda i, j: (i, 0)),
            pl.BlockSpec((None, 1, d), lambda i, j: (layer, 0, 0)),
            pl.BlockSpec((None, d, tn), lambda i, j: (layer, 0, j)),
        ],
        out_specs=pl.BlockSpec((tn // LANES, tm, LANES), lambda i, j: (j, i, 0)),
        out_shape=jax.ShapeDtypeStruct((n // LANES, m, LANES), BF16),
        scratch_shapes=[pltpu.VMEM((tm, d), BF16)],
        compiler_params=_cparams("parallel", "arbitrary"),
        name="norm_matmul",
    )(x2, gain3, w_bf16)


IN_TILE_Q, IN_TILE_FORGET, IN_TILE_GATE = 0, 1, 2


def _in_proj_kernel(x_ref, g_ref, w_ref, lb_ref, z_ref, k_ref, h_ref):
    _normalise_rows(x_ref, g_ref, h_ref)
    j = pl.program_id(1)
    half = z_ref.shape[0] // 2

    has_silu = jnp.logical_or(j == IN_TILE_Q, j == IN_TILE_GATE)

    @pl.when(has_silu)
    def _():
        scale = jnp.where(j == IN_TILE_Q, HG_DIM ** -0.5, 1.0)
        for cb0, acc in _sub_tiles(h_ref, w_ref):
            for jj in range(PROJ_SUB // LANES):
                z = acc[:, jj * LANES:(jj + 1) * LANES]
                if cb0 + jj < half:
                    z = z * _sigmoid(z) * scale
                z_ref[cb0 + jj] = z.astype(z_ref.dtype)

    @pl.when(j > IN_TILE_GATE)
    def _():
        for cb0, acc in _sub_tiles(h_ref, w_ref):
            for jj in range(PROJ_SUB // LANES):
                z_ref[cb0 + jj] = acc[:, jj * LANES:(jj + 1) * LANES].astype(z_ref.dtype)

    @pl.when(j == IN_TILE_FORGET)
    def _():
        for cb0, acc in _sub_tiles(h_ref, w_ref):
            for jj in range(PROJ_SUB // LANES):
                z = acc[:, jj * LANES:(jj + 1) * LANES]
                cb = cb0 + jj
                t = jnp.exp(-jnp.abs(z))
                r = 1.0 / (1.0 + t)
                sig_neg = jnp.where(z >= 0.0, t * r, r)
                k_ref[cb] = (1.0 - lb_ref[:, cb * LANES:(cb + 1) * LANES]) * sig_neg


def _in_proj(x2, gain3, w_bf16, lb3, layer, tm, tn):
    m, d = x2.shape
    n_tiles = w_bf16.shape[-1] // tn
    ncb = tn // LANES
    z_tile = lambda j: j - (j >= IN_TILE_FORGET).astype(jnp.int32)
    return pl.pallas_call(
        _in_proj_kernel,
        grid=(m // tm, n_tiles),
        in_specs=[
            pl.BlockSpec((tm, d), lambda i, j: (i, 0)),
            pl.BlockSpec((None, 1, d), lambda i, j: (layer, 0, 0)),
            pl.BlockSpec((None, d, tn), lambda i, j: (layer, 0, j)),
            pl.BlockSpec((None, 1, tn), lambda i, j: (layer, 0, 0)),
        ],
        out_specs=[pl.BlockSpec((ncb, tm, LANES), lambda i, j: (z_tile(j), i, 0)),
                   pl.BlockSpec((ncb, tm, LANES), lambda i, j: (0, i, 0))],
        out_shape=[jax.ShapeDtypeStruct(((n_tiles - 1) * ncb, m, LANES), BF16),
                   jax.ShapeDtypeStruct((ncb, m, LANES), F32)],
        scratch_shapes=[pltpu.VMEM((tm, d), BF16)],
        compiler_params=_cparams("parallel", "arbitrary"),
        name="in_proj",
    )(x2, gain3, w_bf16, lb3)


def _slabs(x):
    return [x[i:i + SUBLANES, :] for i in range(0, x.shape[0], SUBLANES)]


def _join(slabs):
    return jnp.concatenate(slabs, axis=0)


def _decayed_operands(q, k, f, row8, reverse):
    n = len(f)
    qe, kx, tot = [q[i] * f[i] for i in range(n)], list(k), list(f)
    levels = {1: (qe, kx)}
    b = 1
    while b < SUBLANES:
        later = (row8 & b) != 0
        if reverse:
            later = jnp.logical_not(later)
        back, fwd = (SUBLANES - b, b) if reverse else (b, SUBLANES - b)
        prev = [pltpu.roll(t, back, 0) for t in tot]
        nxt = prev if 2 * b == SUBLANES else [pltpu.roll(t, fwd, 0) for t in tot]
        qe = [qe[i] * jnp.where(later, prev[i], 1.0) for i in range(n)]
        kx = [kx[i] * jnp.where(later, 1.0, nxt[i]) for i in range(n)]
        if 2 * b == SUBLANES:
            tot = [tot[i] * prev[i] for i in range(n)]
        else:
            tot = [tot[i] * jnp.where(later, prev[i], nxt[i]) for i in range(n)]
        b *= 2
        levels[b] = (qe, kx)
    m = 1
    while m < n:
        qe, kx, nt = list(qe), list(kx), []
        for p in range(0, n, 2 * m):
            first, second = (p + m, p) if reverse else (p, p + m)
            t_first, t_second = tot[first // m], tot[second // m]
            for i in range(m):
                qe[second + i] = qe[second + i] * t_first
                kx[first + i] = kx[first + i] * t_second
            nt.append(t_first * t_second)
        tot = nt
        m *= 2
        levels[m * SUBLANES] = (qe, kx)
    return levels, tot[0]


def _hgrn2_kernel(q_ref, v_ref, g_ref, kf_ref, kb_ref, gn_ref, o_ref,
                  oi_ref, qf_ref, qb_ref, kvf_ref, kvb_ref, df_ref, db_ref, sf_ref, sb_ref):
    c = HG_CHUNK
    nc = q_ref.shape[0] // c
    nt = (((1,), (1,)), ((), ()))
    tn = (((0,), (0,)), ((), ()))
    row8 = lax.broadcasted_iota(jnp.int32, (SUBLANES, LANES), 0)
    pair_xor = (lax.broadcasted_iota(jnp.int32, (c, c), 0)
                ^ lax.broadcasted_iota(jnp.int32, (c, c), 1))
    n_slab = c // SUBLANES

    def intra(ci, carry):
        rows = pl.ds(pl.multiple_of(ci * c, c), c)
        q = _slabs(q_ref[rows, :].astype(F32))
        v = v_ref[rows, :]
        k_f = _slabs(kf_ref[rows, :])
        k_b = _slabs(kb_ref[rows, :])
        f_f = [jnp.maximum(1.0 - k, F_FLOOR) for k in k_f]
        f_b = [jnp.maximum(1.0 - k, F_FLOOR) for k in k_b]
        lev_f, tot_f = _decayed_operands(q, k_f, f_f, row8, False)
        lev_b, tot_b = _decayed_operands(q, k_b, f_b, row8, True)

        scores = None
        b = c // 2
        while b >= 1:
            (q_f, x_f), (q_b, x_b) = lev_f[b], lev_b[b]
            if b >= SUBLANES:
                is_right = [(i // (b // SUBLANES)) % 2 == 1 for i in range(n_slab)]
                qh = [q_f[i] if is_right[i] else q_b[i] for i in range(n_slab)]
                kh = [x_b[i] if is_right[i] else x_f[i] for i in range(n_slab)]
            else:
                right = (row8 & b) != 0
                qh = [jnp.where(right, q_f[i], q_b[i]) for i in range(n_slab)]
                kh = [jnp.where(right, x_b[i], x_f[i]) for i in range(n_slab)]
            r = lax.dot_general(_join(qh).astype(BF16), _join(kh).astype(BF16), nt,
                                preferred_element_type=F32)
            scores = r if scores is None else jnp.where(pair_xor < 2 * b, r, scores)
            b //= 2
        k_both = [k_f[i] + k_b[i] for i in range(n_slab)]
        r = lax.dot_general(q_ref[rows, :], _join(k_both).astype(BF16), nt,
                            preferred_element_type=F32)
        scores = jnp.where(pair_xor < 1, r, scores)
        oi_ref[rows, :] = jnp.dot(scores.astype(BF16), v, preferred_element_type=F32)

        (q_f, x_f), (q_b, x_b) = lev_f[c], lev_b[c]
        qf_ref[rows, :] = _join(q_f).astype(BF16)
        qb_ref[rows, :] = _join(q_b).astype(BF16)
        kvf_ref[ci] = lax.dot_general(v, _join(x_f).astype(BF16), tn, preferred_element_type=F32)
        kvb_ref[ci] = lax.dot_general(v, _join(x_b).astype(BF16), tn, preferred_element_type=F32)
        df_ref[ci] = tot_f
        db_ref[ci] = tot_b
        return carry

    lax.fori_loop(0, nc, intra, 0, unroll=2)

    def scan_states(i, carry):
        s_f, s_b = carry
        cb = nc - 1 - i
        sf_ref[i] = s_f.astype(BF16)
        sb_ref[cb] = s_b.astype(BF16)
        s_f = s_f * df_ref[i][0:1, :] + kvf_ref[i]
        s_b = s_b * db_ref[cb][0:1, :] + kvb_ref[cb]
        return s_f, s_b

    zero = jnp.zeros((HG_DIM, HG_DIM), F32)
    lax.fori_loop(0, nc, scan_states, (zero, zero))

    gn = gn_ref[...]

    def finish(ci, carry):
        rows = pl.ds(pl.multiple_of(ci * c, c), c)
        o = oi_ref[rows, :]
        o = o + lax.dot_general(qf_ref[rows, :], sf_ref[ci], nt, preferred_element_type=F32)
        o = o + lax.dot_general(qb_ref[rows, :], sb_ref[ci], nt, preferred_element_type=F32)
        ms = jnp.mean(o * o, axis=-1, keepdims=True)
        o = o * lax.rsqrt(ms + EPS) * gn
        o_ref[rows, :] = (o * g_ref[rows, :].astype(F32)).astype(o_ref.dtype)
        return carry

    lax.fori_loop(0, nc, finish, 0, unroll=4)


def _hgrn2(z3, k3, gnorm3, layer, batch, seq):
    c = HG_CHUNK
    nc = seq // c
    blk = lambda cb0: pl.BlockSpec((None, seq, LANES), lambda b, h: (cb0 + h, b, 0))
    return pl.pallas_call(
        _hgrn2_kernel,
        grid=(batch, HG_HEADS),
        in_specs=[
            blk(CB_HQ), blk(CB_HI), blk(CB_HG),
            blk(0), blk(HG_HEADS),
            pl.BlockSpec((None, 1, LANES), lambda b, h: (layer, 0, 0)),
        ],
        out_specs=pl.BlockSpec((None, seq, LANES), lambda b, h: (h, b, 0)),
        out_shape=jax.ShapeDtypeStruct((HG_HEADS, batch * seq, LANES), BF16),
        scratch_shapes=[
            pltpu.VMEM((seq, LANES), F32),
            pltpu.VMEM((seq, LANES), BF16),
            pltpu.VMEM((seq, LANES), BF16),
            pltpu.VMEM((nc, HG_DIM, HG_DIM), F32),
            pltpu.VMEM((nc, HG_DIM, HG_DIM), F32),
            pltpu.VMEM((nc, SUBLANES, LANES), F32),
            pltpu.VMEM((nc, SUBLANES, LANES), F32),
            pltpu.VMEM((nc, HG_DIM, HG_DIM), BF16),
            pltpu.VMEM((nc, HG_DIM, HG_DIM), BF16),
        ],
        compiler_params=_cparams("parallel", "parallel"),
        name="hgrn2",
    )(z3, z3, z3, k3, k3, gnorm3)


def _natten_kernel(q_ref, k_ref, v_ref, bias_ref, o_ref):
    rows = q_ref.shape[0] // GRID_W
    kh = min(NA_KH, rows)
    win = kh * GRID_W
    nt = (((1,), (1,)), ((), ()))
    lane = lax.broadcasted_iota(jnp.int32, (GRID_W, LANES), 1)
    first = lane < NA_DIM

    def group(gi, carry):
        scored = []
        for u in range(NA_GROUP):
            r = gi * NA_GROUP + u
            rs = jnp.clip(r - kh // 2, 0, rows - kh)
            q2 = q_ref[pl.ds(pl.multiple_of(r * GRID_W, GRID_W), GRID_W), :] * (NA_DIM ** -0.5)
            zero = jnp.zeros_like(q2)
            q_st = jnp.concatenate([jnp.where(first, q2, zero), jnp.where(first, zero, q2)], axis=0)
            kw = k_ref[pl.ds(pl.multiple_of(rs * GRID_W, GRID_W), win), :]
            scored.append((lax.dot_general(q_st, kw, nt, preferred_element_type=F32), r, rs))
        probs = []
        for s, r, rs in scored:
            bm = jnp.concatenate([bias_ref[0, r - rs], bias_ref[1, r - rs]], axis=0)
            s = jnp.where(bm > 0.5 * MASK_NEG, s + bm, MASK_NEG)
            m = jnp.max(s, axis=-1, keepdims=True)
            p = jnp.exp(s - m)
            probs.append((p.astype(BF16), jnp.sum(p, axis=-1, keepdims=True), r, rs))
        for p, l, r, rs in probs:
            vw = v_ref[pl.ds(pl.multiple_of(rs * GRID_W, GRID_W), win), :]
            o_st = jnp.dot(p, vw, preferred_element_type=F32) / l
            o = jnp.where(first, o_st[:GRID_W], o_st[GRID_W:])
            o_ref[pl.ds(pl.multiple_of(r * GRID_W, GRID_W), GRID_W), :] = o.astype(o_ref.dtype)
        return carry

    lax.fori_loop(0, rows // NA_GROUP, group, 0)


def _natten_bias_table(rpb, rows):
    kh = min(NA_KH, rows)
    r = np.arange(rows)
    rs = np.clip(r - kh // 2, 0, rows - kh)
    n_case = int((r - rs).max()) + 1
    case = np.arange(n_case)
    dr = np.arange(kh)[None, :] - case[:, None] + (NA_KH - 1)
    c = np.arange(GRID_W)
    col_start = np.clip(c - NA_KW // 2, 0, GRID_W - NA_KW)
    col_mask = (c[None, :] >= col_start[:, None]) & (c[None, :] < col_start[:, None] + NA_KW)
    dc = np.clip(c[None, :] - c[:, None], -(NA_KW - 1), NA_KW - 1) + (NA_KW - 1)
    depth, heads, _, n_dc = rpb.shape
    by_row = jnp.take(rpb.astype(F32).reshape(depth * heads, -1, n_dc), dr.reshape(-1), axis=1)
    by_row = by_row.reshape(depth * heads, n_case, 1, kh, 1, n_dc)
    tab = jnp.full((depth * heads, n_case, GRID_W, kh, GRID_W), MASK_NEG, F32)
    for d in range(n_dc):
        hit = ((dc == d) & col_mask)[None, None, :, None, :]
        tab = jnp.where(hit, by_row[..., d], tab)
    return tab.reshape(depth, heads, n_case, GRID_W, kh * GRID_W)


def _natten(z3, bias_tab, layer, batch, seq):
    n_case, win = bias_tab.shape[2], bias_tab.shape[4]
    blk = lambda cb0: pl.BlockSpec((None, seq, LANES), lambda b, p: (cb0 + p, b, 0))
    return pl.pallas_call(
        _natten_kernel,
        grid=(batch, NA_HEADS // 2),
        in_specs=[
            blk(CB_NQ), blk(CB_NK), blk(CB_NV),
            pl.BlockSpec((None, 2, n_case, GRID_W, win), lambda b, p: (layer, p, 0, 0, 0)),
        ],
        out_specs=pl.BlockSpec((None, seq, LANES), lambda b, p: (p, b, 0)),
        out_shape=jax.ShapeDtypeStruct((NA_HEADS // 2, batch * seq, LANES), BF16),
        compiler_params=_cparams("parallel", "parallel"),
        name="natten",
    )(z3, z3, z3, bias_tab)


def _memattn_kernel(q_ref, kv_ref, o_ref):
    nt = (((1,), (1,)), ((), ()))
    for h in range(CA_HEADS):
        s = lax.dot_general(q_ref[h], kv_ref[h], nt, preferred_element_type=F32) * (CA_DIM ** -0.5)
        m = jnp.max(s, axis=-1, keepdims=True)
        p = jnp.exp(s - m)
        l = jnp.sum(p, axis=-1, keepdims=True)
        o = jnp.dot(p.astype(BF16), kv_ref[CA_HEADS + h], preferred_element_type=F32) / l
        o_ref[h] = o.astype(o_ref.dtype)


def _memattn(z3, kv3, batch, seq, mem_len, tq):
    nq = seq // tq
    return pl.pallas_call(
        _memattn_kernel,
        grid=(batch, nq),
        in_specs=[
            pl.BlockSpec((CA_HEADS, tq, LANES), lambda b, i: (CB_CQ // CA_HEADS, b * nq + i, 0)),
            pl.BlockSpec((2 * CA_HEADS, mem_len, LANES), lambda b, i: (0, b, 0)),
        ],
        out_specs=pl.BlockSpec((CA_HEADS, tq, LANES), lambda b, i: (0, b * nq + i, 0)),
        out_shape=jax.ShapeDtypeStruct((CA_HEADS, batch * seq, LANES), BF16),
        compiler_params=_cparams("parallel", "parallel"),
        name="memattn",
    )(z3, kv3)


def _cat(ref):
    return jnp.concatenate([ref[j] for j in range(ref.shape[0])], axis=-1)


def _merge_kernel(x_ref, ohg_ref, ona_ref, oca_ref, ghg_ref, gna_ref, gca_ref,
                  whg_ref, wna_ref, wca_ref, wout_ref, o_ref):
    def branch(o3_ref, w_ref, g_ref):
        y = jnp.dot(_cat(o3_ref), w_ref[...], preferred_element_type=F32)
        return _sigmoid(_cat(g_ref).astype(F32)) * y

    merged = branch(ohg_ref, whg_ref, ghg_ref)
    merged = merged + branch(ona_ref, wna_ref, gna_ref)
    merged = merged + branch(oca_ref, wca_ref, gca_ref)
    o_ref[...] = x_ref[...] + jnp.dot(merged.astype(BF16), wout_ref[...],
                                      preferred_element_type=F32)


def _merge(x2, ohg3, ona3, oca3, z3, w_hg_o, w_na_o, w_ca_o, w_out, layer, tm):
    m, d = x2.shape
    ncb = d // LANES
    act = lambda n: pl.BlockSpec((n, tm, LANES), lambda i: (0, i, 0))
    gate = lambda cb0: pl.BlockSpec((ncb, tm, LANES), lambda i: (cb0 // ncb, i, 0))
    full = lambda a: pl.BlockSpec((None,) + a.shape[1:], lambda i: (layer, 0, 0))
    return pl.pallas_call(
        _merge_kernel,
        grid=(m // tm,),
        in_specs=[
            pl.BlockSpec((tm, d), lambda i: (i, 0)),
            act(ohg3.shape[0]), act(ona3.shape[0]), act(oca3.shape[0]),
            gate(CB_GHG), gate(CB_GNA), gate(CB_GCA),
            full(w_hg_o), full(w_na_o), full(w_ca_o), full(w_out),
        ],
        out_specs=pl.BlockSpec((tm, d), lambda i: (i, 0)),
        out_shape=jax.ShapeDtypeStruct((m, d), F32),
        compiler_params=_cparams("parallel"),
        name="merge",
    )(x2, ohg3, ona3, oca3, z3, z3, z3, w_hg_o, w_na_o, w_ca_o, w_out)


def _ffn_kernel(x_ref, xp_ref, xn_ref, gain_ref, wup_ref, cw_ref, cb_ref, wd_ref, fin_ref, o_ref,
                h_ref, y_ref, *, tiles_per_seq, final_norm, ts):
    i = pl.program_id(0)
    tm = x_ref.shape[0]
    halo = SUBLANES

    def normed(x):
        ms = jnp.mean(x * x, axis=-1, keepdims=True)
        return x * lax.rsqrt(ms + EPS) * gain_ref[...]

    keep_prev = (i % tiles_per_seq != 0).astype(F32)
    keep_next = (i % tiles_per_seq != tiles_per_seq - 1).astype(F32)
    h_ref[0:halo, :] = (normed(xp_ref[...]) * keep_prev).astype(BF16)
    h_ref[halo:halo + tm, :] = normed(x_ref[...]).astype(BF16)
    h_ref[halo + tm:, :] = (normed(xn_ref[...]) * keep_next).astype(BF16)

    def conv_proj(cols):
        u = jnp.dot(h_ref[...], wup_ref[:, cols], preferred_element_type=F32)
        out = u[halo - 1:halo - 1 + tm] * cw_ref[0:1, cols]
        out = out + u[halo:halo + tm] * cw_ref[1:2, cols]
        out = out + u[halo + 1:halo + 1 + tm] * cw_ref[2:3, cols]
        return out + cb_ref[:, cols]

    def sub_tile(s, carry):
        cols_a = pl.ds(pl.multiple_of(s * ts, ts), ts)
        cols_g = pl.ds(pl.multiple_of(D_FF + s * ts, ts), ts)
        a = conv_proj(cols_a)
        g = conv_proj(cols_g)
        cdf = 0.5 * (1.0 + jnp.tanh(np.float32(np.sqrt(2 / np.pi)) * (a + 0.044715 * (a * a * a))))
        y_ref[:, cols_a] = (a * cdf * g).astype(BF16)
        return carry

    lax.fori_loop(0, D_FF // ts, sub_tile, 0, unroll=True)

    out = x_ref[...] + jnp.dot(y_ref[...], wd_ref[...], preferred_element_type=F32)
    if final_norm:
        ms = jnp.mean(out * out, axis=-1, keepdims=True)
        out = out * lax.rsqrt(ms + EPS) * fin_ref[...]
    o_ref[...] = out


def _conv_ffn(x2, gain3, w_up, conv_w, conv_b3, w_down, fin_gain, layer, seq, tm, ts, final_norm):
    m, d = x2.shape
    hb = tm // SUBLANES
    n_halo = m // SUBLANES
    kern = functools.partial(_ffn_kernel, tiles_per_seq=seq // tm, final_norm=final_norm, ts=ts)
    resident = lambda a: pl.BlockSpec((None,) + a.shape[1:], lambda i: (layer, 0, 0),
                                      pipeline_mode=pl.Buffered(1))
    return pl.pallas_call(
        kern,
        grid=(m // tm,),
        in_specs=[
            pl.BlockSpec((tm, d), lambda i: (i, 0)),
            pl.BlockSpec((SUBLANES, d), lambda i: (jnp.maximum(i * hb - 1, 0), 0)),
            pl.BlockSpec((SUBLANES, d), lambda i: (jnp.minimum((i + 1) * hb, n_halo - 1), 0)),
            pl.BlockSpec((None, 1, d), lambda i: (layer, 0, 0)),
            resident(w_up), resident(conv_w), resident(conv_b3), resident(w_down),
            pl.BlockSpec((1, d), lambda i: (0, 0)),
        ],
        out_specs=pl.BlockSpec((tm, d), lambda i: (i, 0)),
        out_shape=jax.ShapeDtypeStruct((m, d), F32),
        scratch_shapes=[
            pltpu.VMEM((tm + 2 * SUBLANES, d), BF16),
            pltpu.VMEM((tm, D_FF), BF16),
        ],
        compiler_params=_cparams("parallel"),
        name="conv_ffn",
    )(x2, x2, x2, gain3, w_up, conv_w, conv_b3, w_down, fin_gain.reshape(1, d))


def kernel(x, mem, norm_mix, w_in, hg_lb_logits, hg_gnorm, w_hg_o, na_rpb, w_na_o, mem_norm,
           w_mem_kv, w_ca_o, w_out, norm_ffn, w_up, conv_w, conv_b, w_down, norm_final):
    batch, seq, d = x.shape
    mem_len = mem.shape[1]
    depth = w_in.shape[0]
    assert d == D_MODEL and seq % GRID_W == 0 and seq % HG_CHUNK == 0
    hgw = HG_HEADS * HG_DIM

    p_lb = jax.nn.softmax(hg_lb_logits.astype(F32), axis=0)
    lower_bounds = jnp.clip(jnp.cumsum(p_lb, axis=0) - p_lb[0], 0.0, 1.0)

    bf = lambda a: a.astype(BF16)
    w_in, w_mem_kv, w_hg_o, w_na_o, w_ca_o, w_out, w_up, w_down = map(
        bf, (w_in, w_mem_kv, w_hg_o, w_na_o, w_ca_o, w_out, w_up, w_down))
    vec3 = lambda a: a.reshape(depth, 1, -1)
    norm_mix3, norm_ffn3, gnorm3, conv_b3 = map(vec3, (norm_mix, norm_ffn, hg_gnorm, conv_b))
    lb3 = vec3(lower_bounds)
    mem_norm3 = jnp.broadcast_to(mem_norm.reshape(1, 1, d), (depth, 1, d))
    bias_tab = _natten_bias_table(na_rpb, seq // GRID_W)

    assert w_in.shape[-1] == (N_CB + 2 * hgw // LANES) * LANES

    x2 = x.reshape(batch * seq, d)
    mem2 = mem.reshape(batch * mem_len, d)

    for l in range(depth):
        z3, k3 = _in_proj(x2, norm_mix3, w_in, lb3, l, tm=1024, tn=2 * hgw)
        kv3 = _norm_matmul(mem2, mem_norm3, w_mem_kv, l, tm=batch * mem_len,
                           tn=w_mem_kv.shape[-1])

        ohg3 = _hgrn2(z3, k3, gnorm3, l, batch, seq)
        ona3 = _natten(z3, bias_tab, l, batch, seq)
        oca3 = _memattn(z3, kv3, batch, seq, mem_len, tq=512)

        x2 = _merge(x2, ohg3, ona3, oca3, z3, w_hg_o, w_na_o, w_ca_o, w_out, l, tm=512)
        x2 = _conv_ffn(x2, norm_ffn3, w_up, conv_w, conv_b3, w_down, norm_final, l, seq,
                       tm=512, ts=256, final_norm=(l == depth - 1))
    return x2.reshape(batch, seq, d)
```

```python
import functools

import numpy as np
import jax
import jax.numpy as jnp
from jax import lax
from jax.experimental import pallas as pl
from jax.experimental.pallas import tpu as pltpu

D_MODEL = 1024
GRID_W = 64
HG_HEADS = 8
HG_DIM = 128
NA_HEADS = 8
NA_DIM = 64
NA_KH = 8
NA_KW = 16
CA_HEADS = 4
CA_DIM = 128
D_FF = 2816
CONV_W = 3
EPS = 1e-6
F_FLOOR = 1e-12
MASK_NEG = -1e30

LANES = 128
SUBLANES = 8
VMEM_LIMIT = 56 * 1024 * 1024

HG_CHUNK = 128
NA_GROUP = 4

CB_HQ, CB_HI, CB_HG = 0, 8, 16
CB_NQ, CB_NK, CB_NV = 24, 28, 32
CB_CQ = 36
CB_GHG, CB_GNA, CB_GCA = 40, 48, 56
N_CB = 64

F32 = jnp.float32
BF16 = jnp.bfloat16


def _cparams(*sem):
    return pltpu.CompilerParams(dimension_semantics=sem, vmem_limit_bytes=VMEM_LIMIT)


def _sigmoid(x):
    return 1.0 / (1.0 + jnp.exp(-x))


PROJ_SUB = 512


def _normalise_rows(x_ref, g_ref, h_ref):
    @pl.when(pl.program_id(1) == 0)
    def _():
        x = x_ref[...]
        ms = jnp.mean(x * x, axis=-1, keepdims=True)
        h_ref[...] = (x * lax.rsqrt(ms + EPS) * g_ref[...]).astype(BF16)


def _sub_tiles(h_ref, w_ref):
    for s in range(w_ref.shape[1] // PROJ_SUB):
        acc = jnp.dot(h_ref[...], w_ref[:, s * PROJ_SUB:(s + 1) * PROJ_SUB],
                      preferred_element_type=F32)
        yield s * (PROJ_SUB // LANES), acc


def _norm_mm_kernel(x_ref, g_ref, w_ref, o_ref, h_ref):
    _normalise_rows(x_ref, g_ref, h_ref)
    for cb0, acc in _sub_tiles(h_ref, w_ref):
        for jj in range(PROJ_SUB // LANES):
            o_ref[cb0 + jj] = acc[:, jj * LANES:(jj + 1) * LANES].astype(o_ref.dtype)


def _norm_matmul(x2, gain3, w_bf16, layer, tm, tn):
    m, d = x2.shape
    n = w_bf16.shape[-1]
    return pl.pallas_call(
        _norm_mm_kernel,
        grid=(m // tm, n // tn),
        in_specs=[
            pl.BlockSpec((tm, d), lambda i, j: (i, 0)),
            pl.BlockSpec((None, 1, d), lambda i, j: (layer, 0, 0)),
            pl.BlockSpec((None, d, tn), lambda i, j: (layer, 0, j)),
        ],
        out_specs=pl.BlockSpec((tn // LANES, tm, LANES), lambda i, j: (j, i, 0)),
        out_shape=jax.ShapeDtypeStruct((n // LANES, m, LANES), BF16),
        scratch_shapes=[pltpu.VMEM((tm, d), BF16)],
        compiler_params=_cparams("parallel", "arbitrary"),
        name="norm_matmul",
    )(x2, gain3, w_bf16)


IN_TILE_Q, IN_TILE_FORGET, IN_TILE_GATE = 0, 1, 2


def _in_proj_kernel(x_ref, g_ref, w_ref, lb_ref, z_ref, k_ref, h_ref):
    _normalise_rows(x_ref, g_ref, h_ref)
    j = pl.program_id(1)
    half = z_ref.shape[0] // 2

    has_silu = jnp.logical_or(j == IN_TILE_Q, j == IN_TILE_GATE)

    @pl.when(has_silu)
    def _():
        scale = jnp.where(j == IN_TILE_Q, HG_DIM ** -0.5, 1.0)
        for cb0, acc in _sub_tiles(h_ref, w_ref):
            for jj in range(PROJ_SUB // LANES):
                z = acc[:, jj * LANES:(jj + 1) * LANES]
                if cb0 + jj < half:
                    z = z * _sigmoid(z) * scale
                z_ref[cb0 + jj] = z.astype(z_ref.dtype)

    @pl.when(j > IN_TILE_GATE)
    def _():
        for cb0, acc in _sub_tiles(h_ref, w_ref):
            for jj in range(PROJ_SUB // LANES):
                z_ref[cb0 + jj] = acc[:, jj * LANES:(jj + 1) * LANES].astype(z_ref.dtype)

    @pl.when(j == IN_TILE_FORGET)
    def _():
        for cb0, acc in _sub_tiles(h_ref, w_ref):
            for jj in range(PROJ_SUB // LANES):
                z = acc[:, jj * LANES:(jj + 1) * LANES]
                cb = cb0 + jj
                t = jnp.exp(-jnp.abs(z))
                r = 1.0 / (1.0 + t)
                sig_neg = jnp.where(z >= 0.0, t * r, r)
                k_ref[cb] = (1.0 - lb_ref[:, cb * LANES:(cb + 1) * LANES]) * sig_neg


def _in_proj(x2, gain3, w_bf16, lb3, layer, tm, tn):
    m, d = x2.shape
    n_tiles = w_bf16.shape[-1] // tn
    ncb = tn // LANES
    z_tile = lambda j: j - (j >= IN_TILE_FORGET).astype(jnp.int32)
    return pl.pallas_call(
        _in_proj_kernel,
        grid=(m // tm, n_tiles),
        in_specs=[
            pl.BlockSpec((tm, d), lambda i, j: (i, 0)),
            pl.BlockSpec((None, 1, d), lambda i, j: (layer, 0, 0)),
            pl.BlockSpec((None, d, tn), lambda i, j: (layer, 0, j)),
            pl.BlockSpec((None, 1, tn), lambda i, j: (layer, 0, 0)),
        ],
        out_specs=[pl.BlockSpec((ncb, tm, LANES), lambda i, j: (z_tile(j), i, 0)),
                   pl.BlockSpec((ncb, tm, LANES), lambda i, j: (0, i, 0))],
        out_shape=[jax.ShapeDtypeStruct(((n_tiles - 1) * ncb, m, LANES), BF16),
                   jax.ShapeDtypeStruct((ncb, m, LANES), F32)],
        scratch_shapes=[pltpu.VMEM((tm, d), BF16)],
        compiler_params=_cparams("parallel", "arbitrary"),
        name="in_proj",
    )(x2, gain3, w_bf16, lb3)


def _slabs(x):
    return [x[i:i + SUBLANES, :] for i in range(0, x.shape[0], SUBLANES)]


def _join(slabs):
    return jnp.concatenate(slabs, axis=0)


def _double_blocks(qe, kx, tot, b, row8, reverse):
    n = len(qe)
    if b < SUBLANES:
        later = (row8 & b) != 0
        if reverse:
            later = jnp.logical_not(later)
        back, fwd = (SUBLANES - b, b) if reverse else (b, SUBLANES - b)
        prev = [pltpu.roll(t, back, 0) for t in tot]
        nxt = prev if 2 * b == SUBLANES else [pltpu.roll(t, fwd, 0) for t in tot]
        qe = [qe[i] * jnp.where(later, prev[i], 1.0) for i in range(n)]
        kx = [kx[i] * jnp.where(later, 1.0, nxt[i]) for i in range(n)]
        if 2 * b == SUBLANES:
            tot = [tot[i] * prev[i] for i in range(n)]
        else:
            tot = [tot[i] * jnp.where(later, prev[i], nxt[i]) for i in range(n)]
        return qe, kx, tot
    m = b // SUBLANES
    qe, kx, merged = list(qe), list(kx), []
    for p in range(0, n, 2 * m):
        first, second = (p + m, p) if reverse else (p, p + m)
        t_first, t_second = tot[first // m], tot[second // m]
        for i in range(m):
            qe[second + i] = qe[second + i] * t_first
            kx[first + i] = kx[first + i] * t_second
        merged.append(t_first * t_second)
    return qe, kx, merged


def _hgrn2_kernel(q_ref, v_ref, g_ref, kf_ref, kb_ref, gn_ref, o_ref,
                  oi_ref, qf_ref, qb_ref, kvf_ref, kvb_ref, df_ref, db_ref, sf_ref, sb_ref):
    c = HG_CHUNK
    nc = q_ref.shape[0] // c
    nt = (((1,), (1,)), ((), ()))
    tn = (((0,), (0,)), ((), ()))
    row8 = lax.broadcasted_iota(jnp.int32, (SUBLANES, LANES), 0)
    pair_xor = (lax.broadcasted_iota(jnp.int32, (c, c), 0)
                ^ lax.broadcasted_iota(jnp.int32, (c, c), 1))
    n_slab = c // SUBLANES

    def intra(ci, carry):
        rows = pl.ds(pl.multiple_of(ci * c, c), c)
        q = _slabs(q_ref[rows, :].astype(F32))
        v = v_ref[rows, :]
        k_f = _slabs(kf_ref[rows, :])
        k_b = _slabs(kb_ref[rows, :])
        f_f = [jnp.maximum(1.0 - k, F_FLOOR) for k in k_f]
        f_b = [jnp.maximum(1.0 - k, F_FLOOR) for k in k_b]
        fwd = ([q[i] * f_f[i] for i in range(n_slab)], k_f, f_f)
        bwd = ([q[i] * f_b[i] for i in range(n_slab)], k_b, f_b)

        k_both = [k_f[i] + k_b[i] for i in range(n_slab)]
        scores = lax.dot_general(q_ref[rows, :], _join(k_both).astype(BF16), nt,
                                 preferred_element_type=F32)
        b = 1
        while b < c:
            (q_f, x_f, _), (q_b, x_b, _) = fwd, bwd
            if b >= SUBLANES:
                is_right = [(i // (b // SUBLANES)) % 2 == 1 for i in range(n_slab)]
                qh = [q_f[i] if is_right[i] else q_b[i] for i in range(n_slab)]
                kh = [x_b[i] if is_right[i] else x_f[i] for i in range(n_slab)]
            else:
                right = (row8 & b) != 0
                qh = [jnp.where(right, q_f[i], q_b[i]) for i in range(n_slab)]
                kh = [jnp.where(right, x_b[i], x_f[i]) for i in range(n_slab)]
            r = lax.dot_general(_join(qh).astype(BF16), _join(kh).astype(BF16), nt,
                                preferred_element_type=F32)
            scores = jnp.where(pair_xor >= b, r, scores)
            fwd = _double_blocks(*fwd, b, row8, False)
            bwd = _double_blocks(*bwd, b, row8, True)
            b *= 2
        oi_ref[rows, :] = jnp.dot(scores.astype(BF16), v, preferred_element_type=F32)

        (q_f, x_f, tot_f), (q_b, x_b, tot_b) = fwd, bwd
        qf_ref[rows, :] = _join(q_f).astype(BF16)
        qb_ref[rows, :] = _join(q_b).astype(BF16)
        kvf_ref[ci] = lax.dot_general(v, _join(x_f).astype(BF16), tn, preferred_element_type=F32)
        kvb_ref[ci] = lax.dot_general(v, _join(x_b).astype(BF16), tn, preferred_element_type=F32)
        df_ref[ci] = tot_f[0]
        db_ref[ci] = tot_b[0]
        return carry

    lax.fori_loop(0, nc, intra, 0, unroll=8)

    def scan_states(i, carry):
        s_f, s_b = carry
        cb = nc - 1 - i
        sf_ref[i] = s_f.astype(BF16)
        sb_ref[cb] = s_b.astype(BF16)
        s_f = s_f * df_ref[i][0:1, :] + kvf_ref[i]
        s_b = s_b * db_ref[cb][0:1, :] + kvb_ref[cb]
        return s_f, s_b

    zero = jnp.zeros((HG_DIM, HG_DIM), F32)
    lax.fori_loop(0, nc, scan_states, (zero, zero))

    gn = gn_ref[...]

    def finish(ci, carry):
        rows = pl.ds(pl.multiple_of(ci * c, c), c)
        o = oi_ref[rows, :]
        o = o + lax.dot_general(qf_ref[rows, :], sf_ref[ci], nt, preferred_element_type=F32)
        o = o + lax.dot_general(qb_ref[rows, :], sb_ref[ci], nt, preferred_element_type=F32)
        ms = jnp.mean(o * o, axis=-1, keepdims=True)
        o = o * lax.rsqrt(ms + EPS) * gn
        o_ref[rows, :] = (o * g_ref[rows, :].astype(F32)).astype(o_ref.dtype)
        return carry

    lax.fori_loop(0, nc, finish, 0, unroll=8)


def _hgrn2(z3, k3, gnorm3, layer, batch, seq):
    c = HG_CHUNK
    nc = seq // c
    blk = lambda cb0: pl.BlockSpec((None, seq, LANES), lambda b, h: (cb0 + h, b, 0))
    return pl.pallas_call(
        _hgrn2_kernel,
        grid=(batch, HG_HEADS),
        in_specs=[
            blk(CB_HQ), blk(CB_HI), blk(CB_HG),
            blk(0), blk(HG_HEADS),
            pl.BlockSpec((None, 1, LANES), lambda b, h: (layer, 0, 0)),
        ],
        out_specs=pl.BlockSpec((None, seq, LANES), lambda b, h: (h, b, 0)),
        out_shape=jax.ShapeDtypeStruct((HG_HEADS, batch * seq, LANES), BF16),
        scratch_shapes=[
            pltpu.VMEM((seq, LANES), F32),
            pltpu.VMEM((seq, LANES), BF16),
            pltpu.VMEM((seq, LANES), BF16),
            pltpu.VMEM((nc, HG_DIM, HG_DIM), F32),
            pltpu.VMEM((nc, HG_DIM, HG_DIM), F32),
            pltpu.VMEM((nc, SUBLANES, LANES), F32),
            pltpu.VMEM((nc, SUBLANES, LANES), F32),
            pltpu.VMEM((nc, HG_DIM, HG_DIM), BF16),
            pltpu.VMEM((nc, HG_DIM, HG_DIM), BF16),
        ],
        compiler_params=_cparams("parallel", "parallel"),
        name="hgrn2",
    )(z3, z3, z3, k3, k3, gnorm3)


def _natten_kernel(q_ref, k_ref, v_ref, bias_ref, o_ref):
    rows = q_ref.shape[0] // GRID_W
    kh = min(NA_KH, rows)
    win = kh * GRID_W
    nt = (((1,), (1,)), ((), ()))
    lane = lax.broadcasted_iota(jnp.int32, (GRID_W, LANES), 1)
    first = lane < NA_DIM

    def group(gi, carry):
        scored = []
        for u in range(NA_GROUP):
            r = gi * NA_GROUP + u
            rs = jnp.clip(r - kh // 2, 0, rows - kh)
            q2 = q_ref[pl.ds(pl.multiple_of(r * GRID_W, GRID_W), GRID_W), :] * (NA_DIM ** -0.5)
            zero = jnp.zeros_like(q2)
            q_st = jnp.concatenate([jnp.where(first, q2, zero), jnp.where(first, zero, q2)], axis=0)
            kw = k_ref[pl.ds(pl.multiple_of(rs * GRID_W, GRID_W), win), :]
            scored.append((lax.dot_general(q_st, kw, nt, preferred_element_type=F32), r, rs))
        probs = []
        for s, r, rs in scored:
            bm = jnp.concatenate([bias_ref[0, r - rs], bias_ref[1, r - rs]], axis=0)
            s = jnp.where(bm > 0.5 * MASK_NEG, s + bm, MASK_NEG)
            m = jnp.max(s, axis=-1, keepdims=True)
            p = jnp.exp(s - m)
            probs.append((p.astype(BF16), jnp.sum(p, axis=-1, keepdims=True), r, rs))
        for p, l, r, rs in probs:
            vw = v_ref[pl.ds(pl.multiple_of(rs * GRID_W, GRID_W), win), :]
            o_st = jnp.dot(p, vw, preferred_element_type=F32) / l
            o = jnp.where(first, o_st[:GRID_W], o_st[GRID_W:])
            o_ref[pl.ds(pl.multiple_of(r * GRID_W, GRID_W), GRID_W), :] = o.astype(o_ref.dtype)
        return carry

    lax.fori_loop(0, rows // NA_GROUP, group, 0)


def _natten_bias_table(rpb, rows):
    kh = min(NA_KH, rows)
    r = np.arange(rows)
    rs = np.clip(r - kh // 2, 0, rows - kh)
    n_case = int((r - rs).max()) + 1
    case = np.arange(n_case)
    dr = np.arange(kh)[None, :] - case[:, None] + (NA_KH - 1)
    c = np.arange(GRID_W)
    col_start = np.clip(c - NA_KW // 2, 0, GRID_W - NA_KW)
    col_mask = (c[None, :] >= col_start[:, None]) & (c[None, :] < col_start[:, None] + NA_KW)
    dc = np.clip(c[None, :] - c[:, None], -(NA_KW - 1), NA_KW - 1) + (NA_KW - 1)
    depth, heads, _, n_dc = rpb.shape
    by_row = jnp.take(rpb.astype(F32).reshape(depth * heads, -1, n_dc), dr.reshape(-1), axis=1)
    by_row = by_row.reshape(depth * heads, n_case, kh, n_dc)
    pick = (dc[None, :, :] == np.arange(n_dc)[:, None, None]).astype(np.float32)
    tab = jnp.einsum("xcjd,dqk->xcqjk", by_row, pick, precision=lax.Precision.HIGHEST)
    tab = jnp.where(col_mask[None, None, :, None, :], tab, MASK_NEG)
    return tab.reshape(depth, heads, n_case, GRID_W, kh * GRID_W)


def _natten(z3, bias_tab, layer, batch, seq):
    n_case, win = bias_tab.shape[2], bias_tab.shape[4]
    blk = lambda cb0: pl.BlockSpec((None, seq, LANES), lambda b, p: (cb0 + p, b, 0))
    return pl.pallas_call(
        _natten_kernel,
        grid=(batch, NA_HEADS // 2),
        in_specs=[
            blk(CB_NQ), blk(CB_NK), blk(CB_NV),
            pl.BlockSpec((None, 2, n_case, GRID_W, win), lambda b, p: (layer, p, 0, 0, 0)),
        ],
        out_specs=pl.BlockSpec((None, seq, LANES), lambda b, p: (p, b, 0)),
        out_shape=jax.ShapeDtypeStruct((NA_HEADS // 2, batch * seq, LANES), BF16),
        compiler_params=_cparams("parallel", "parallel"),
        name="natten",
    )(z3, z3, z3, bias_tab)


def _memattn_kernel(q_ref, kv_ref, o_ref):
    nt = (((1,), (1,)), ((), ()))
    for h in range(CA_HEADS):
        s = lax.dot_general(q_ref[h], kv_ref[h], nt, preferred_element_type=F32) * (CA_DIM ** -0.5)
        m = jnp.max(s, axis=-1, keepdims=True)
        p = jnp.exp(s - m)
        l = jnp.sum(p, axis=-1, keepdims=True)
        o = jnp.dot(p.astype(BF16), kv_ref[CA_HEADS + h], preferred_element_type=F32) / l
        o_ref[h] = o.astype(o_ref.dtype)


def _memattn(z3, kv3, batch, seq, mem_len, tq):
    nq = seq // tq
    return pl.pallas_call(
        _memattn_kernel,
        grid=(batch, nq),
        in_specs=[
            pl.BlockSpec((CA_HEADS, tq, LANES), lambda b, i: (CB_CQ // CA_HEADS, b * nq + i, 0)),
            pl.BlockSpec((2 * CA_HEADS, mem_len, LANES), lambda b, i: (0, b, 0)),
        ],
        out_specs=pl.BlockSpec((CA_HEADS, tq, LANES), lambda b, i: (0, b * nq + i, 0)),
        out_shape=jax.ShapeDtypeStruct((CA_HEADS, batch * seq, LANES), BF16),
        compiler_params=_cparams("parallel", "parallel"),
        name="memattn",
    )(z3, kv3)


def _cat(ref):
    return jnp.concatenate([ref[j] for j in range(ref.shape[0])], axis=-1)


def _merge_kernel(x_ref, ohg_ref, ona_ref, oca_ref, ghg_ref, gna_ref, gca_ref,
                  whg_ref, wna_ref, wca_ref, wout_ref, o_ref):
    def branch(o3_ref, w_ref, g_ref):
        y = jnp.dot(_cat(o3_ref), w_ref[...], preferred_element_type=F32)
        return _sigmoid(_cat(g_ref).astype(F32)) * y

    merged = branch(ohg_ref, whg_ref, ghg_ref)
    merged = merged + branch(ona_ref, wna_ref, gna_ref)
    merged = merged + branch(oca_ref, wca_ref, gca_ref)
    o_ref[...] = x_ref[...] + jnp.dot(merged.astype(BF16), wout_ref[...],
                                      preferred_element_type=F32)


def _merge(x2, ohg3, ona3, oca3, z3, w_hg_o, w_na_o, w_ca_o, w_out, layer, tm):
    m, d = x2.shape
    ncb = d // LANES
    act = lambda n: pl.BlockSpec((n, tm, LANES), lambda i: (0, i, 0))
    gate = lambda cb0: pl.BlockSpec((ncb, tm, LANES), lambda i: (cb0 // ncb, i, 0))
    full = lambda a: pl.BlockSpec((None,) + a.shape[1:], lambda i: (layer, 0, 0))
    return pl.pallas_call(
        _merge_kernel,
        grid=(m // tm,),
        in_specs=[
            pl.BlockSpec((tm, d), lambda i: (i, 0)),
            act(ohg3.shape[0]), act(ona3.shape[0]), act(oca3.shape[0]),
            gate(CB_GHG), gate(CB_GNA), gate(CB_GCA),
            full(w_hg_o), full(w_na_o), full(w_ca_o), full(w_out),
        ],
        out_specs=pl.BlockSpec((tm, d), lambda i: (i, 0)),
        out_shape=jax.ShapeDtypeStruct((m, d), F32),
        compiler_params=_cparams("parallel"),
        name="merge",
    )(x2, ohg3, ona3, oca3, z3, z3, z3, w_hg_o, w_na_o, w_ca_o, w_out)


def _ffn_kernel(x_ref, xp_ref, xn_ref, gain_ref, wup_ref, cw_ref, cb_ref, wd_ref, fin_ref, o_ref,
                h_ref, y_ref, *, tiles_per_seq, final_norm, ts):
    i = pl.program_id(0)
    tm = x_ref.shape[0]
    halo = SUBLANES

    def normed(x):
        ms = jnp.mean(x * x, axis=-1, keepdims=True)
        return x * lax.rsqrt(ms + EPS) * gain_ref[...]

    keep_prev = (i % tiles_per_seq != 0).astype(F32)
    keep_next = (i % tiles_per_seq != tiles_per_seq - 1).astype(F32)
    h_ref[0:halo, :] = (normed(xp_ref[...]) * keep_prev).astype(BF16)
    h_ref[halo:halo + tm, :] = normed(x_ref[...]).astype(BF16)
    h_ref[halo + tm:, :] = (normed(xn_ref[...]) * keep_next).astype(BF16)

    def conv_proj(cols):
        u = jnp.dot(h_ref[...], wup_ref[:, cols], preferred_element_type=F32)
        out = u[halo - 1:halo - 1 + tm] * cw_ref[0:1, cols]
        out = out + u[halo:halo + tm] * cw_ref[1:2, cols]
        out = out + u[halo + 1:halo + 1 + tm] * cw_ref[2:3, cols]
        return out + cb_ref[:, cols]

    def sub_tile(s, carry):
        cols_a = pl.ds(pl.multiple_of(s * ts, ts), ts)
        cols_g = pl.ds(pl.multiple_of(D_FF + s * ts, ts), ts)
        a = conv_proj(cols_a)
        g = conv_proj(cols_g)
        cdf = 0.5 * (1.0 + jnp.tanh(np.float32(np.sqrt(2 / np.pi)) * (a + 0.044715 * (a * a * a))))
        y_ref[:, cols_a] = (a * cdf * g).astype(BF16)
        return carry

    lax.fori_loop(0, D_FF // ts, sub_tile, 0, unroll=True)

    out = x_ref[...] + jnp.dot(y_ref[...], wd_ref[...], preferred_element_type=F32)
    if final_norm:
        ms = jnp.mean(out * out, axis=-1, keepdims=True)
        out = out * lax.rsqrt(ms + EPS) * fin_ref[...]
    o_ref[...] = out


def _conv_ffn(x2, gain3, w_up, conv_w, conv_b3, w_down, fin_gain, layer, seq, tm, ts, final_norm):
    m, d = x2.shape
    hb = tm // SUBLANES
    n_halo = m // SUBLANES
    kern = functools.partial(_ffn_kernel, tiles_per_seq=seq // tm, final_norm=final_norm, ts=ts)
    resident = lambda a: pl.BlockSpec((None,) + a.shape[1:], lambda i: (layer, 0, 0),
                                      pipeline_mode=pl.Buffered(1))
    return pl.pallas_call(
        kern,
        grid=(m // tm,),
        in_specs=[
            pl.BlockSpec((tm, d), lambda i: (i, 0)),
            pl.BlockSpec((SUBLANES, d), lambda i: (jnp.maximum(i * hb - 1, 0), 0)),
            pl.BlockSpec((SUBLANES, d), lambda i: (jnp.minimum((i + 1) * hb, n_halo - 1), 0)),
            pl.BlockSpec((None, 1, d), lambda i: (layer, 0, 0)),
            resident(w_up), resident(conv_w), resident(conv_b3), resident(w_down),
            pl.BlockSpec((1, d), lambda i: (0, 0)),
        ],
        out_specs=pl.BlockSpec((tm, d), lambda i: (i, 0)),
        out_shape=jax.ShapeDtypeStruct((m, d), F32),
        scratch_shapes=[
            pltpu.VMEM((tm + 2 * SUBLANES, d), BF16),
            pltpu.VMEM((tm, D_FF), BF16),
        ],
        compiler_params=_cparams("parallel"),
        name="conv_ffn",
    )(x2, x2, x2, gain3, w_up, conv_w, conv_b3, w_down, fin_gain.reshape(1, d))


def kernel(x, mem, norm_mix, w_in, hg_lb_logits, hg_gnorm, w_hg_o, na_rpb, w_na_o, mem_norm,
           w_mem_kv, w_ca_o, w_out, norm_ffn, w_up, conv_w, conv_b, w_down, norm_final):
    batch, seq, d = x.shape
    mem_len = mem.shape[1]
    depth = w_in.shape[0]
    assert d == D_MODEL and seq % GRID_W == 0 and seq % HG_CHUNK == 0
    hgw = HG_HEADS * HG_DIM

    p_lb = jax.nn.softmax(hg_lb_logits.astype(F32), axis=0)
    lower_bounds = jnp.clip(jnp.cumsum(p_lb, axis=0) - p_lb[0], 0.0, 1.0)

    bf = lambda a: a.astype(BF16)
    w_in, w_mem_kv, w_hg_o, w_na_o, w_ca_o, w_out, w_up, w_down = map(
        bf, (w_in, w_mem_kv, w_hg_o, w_na_o, w_ca_o, w_out, w_up, w_down))
    vec3 = lambda a: a.reshape(depth, 1, -1)
    norm_mix3, norm_ffn3, gnorm3, conv_b3 = map(vec3, (norm_mix, norm_ffn, hg_gnorm, conv_b))
    lb3 = vec3(lower_bounds)
    mem_norm3 = jnp.broadcast_to(mem_norm.reshape(1, 1, d), (depth, 1, d))
    bias_tab = _natten_bias_table(na_rpb, seq // GRID_W)

    assert w_in.shape[-1] == (N_CB + 2 * hgw // LANES) * LANES

    x2 = x.reshape(batch * seq, d)
    mem2 = mem.reshape(batch * mem_len, d)

    for l in range(depth):
        z3, k3 = _in_proj(x2, norm_mix3, w_in, lb3, l, tm=1024, tn=2 * hgw)
        kv3 = _norm_matmul(mem2, mem_norm3, w_mem_kv, l, tm=batch * mem_len,
                           tn=w_mem_kv.shape[-1])

        ohg3 = _hgrn2(z3, k3, gnorm3, l, batch, seq)
        ona3 = _natten(z3, bias_tab, l, batch, seq)
        oca3 = _memattn(z3, kv3, batch, seq, mem_len, tq=512)

        x2 = _merge(x2, ohg3, ona3, oca3, z3, w_hg_o, w_na_o, w_ca_o, w_out, l, tm=512)
        x2 = _conv_ffn(x2, norm_ffn3, w_up, conv_w, conv_b3, w_down, norm_final, l, seq,
                       tm=512, ts=256, final_norm=(l == depth - 1))
    return x2.reshape(batch, seq, d)
```

```python
import functools

import numpy as np
import jax
import jax.numpy as jnp
from jax import lax
from jax.experimental import pallas as pl
from jax.experimental.pallas import tpu as pltpu

D_MODEL = 1024
GRID_W = 64
HG_HEADS = 8
HG_DIM = 128
NA_HEADS = 8
NA_DIM = 64
NA_KH = 8
NA_KW = 16
CA_HEADS = 4
CA_DIM = 128
D_FF = 2816
CONV_W = 3
EPS = 1e-6
F_FLOOR = 1e-12
MASK_NEG = -1e30

LANES = 128
SUBLANES = 8
VMEM_LIMIT = 56 * 1024 * 1024

HG_CHUNK = 128
NA_GROUP = 8

CB_HQ, CB_HI, CB_HG = 0, 8, 16
CB_NQ, CB_NK, CB_NV = 24, 28, 32
CB_CQ = 36
CB_GHG, CB_GNA, CB_GCA = 40, 48, 56
N_CB = 64

F32 = jnp.float32
BF16 = jnp.bfloat16


def _cparams(*sem):
    return pltpu.CompilerParams(dimension_semantics=sem, vmem_limit_bytes=VMEM_LIMIT)


def _sigmoid(x):
    return 1.0 / (1.0 + jnp.exp(-x))


PROJ_SUB = 512


def _normalise_rows(x_ref, g_ref, h_ref):
    @pl.when(pl.program_id(1) == 0)
    def _():
        x = x_ref[...]
        ms = jnp.mean(x * x, axis=-1, keepdims=True)
        h_ref[...] = (x * lax.rsqrt(ms + EPS) * g_ref[...]).astype(BF16)


def _sub_tiles(h_ref, w_ref):
    for s in range(w_ref.shape[1] // PROJ_SUB):
        acc = jnp.dot(h_ref[...], w_ref[:, s * PROJ_SUB:(s + 1) * PROJ_SUB],
                      preferred_element_type=F32)
        yield s * (PROJ_SUB // LANES), acc


def _norm_mm_kernel(x_ref, g_ref, w_ref, o_ref, h_ref):
    _normalise_rows(x_ref, g_ref, h_ref)
    for cb0, acc in _sub_tiles(h_ref, w_ref):
        for jj in range(PROJ_SUB // LANES):
            o_ref[cb0 + jj] = acc[:, jj * LANES:(jj + 1) * LANES].astype(o_ref.dtype)


def _norm_matmul(x2, gain3, w_bf16, layer, tm, tn):
    m, d = x2.shape
    n = w_bf16.shape[-1]
    return pl.pallas_call(
        _norm_mm_kernel,
        grid=(m // tm, n // tn),
        in_specs=[
            pl.BlockSpec((tm, d), lambda i, j: (i, 0)),
            pl.BlockSpec((None, 1, d), lambda i, j: (layer, 0, 0)),
            pl.BlockSpec((None, d, tn), lambda i, j: (layer, 0, j)),
        ],
        out_specs=pl.BlockSpec((tn // LANES, tm, LANES), lambda i, j: (j, i, 0)),
        out_shape=jax.ShapeDtypeStruct((n // LANES, m, LANES), BF16),
        scratch_shapes=[pltpu.VMEM((tm, d), BF16)],
        compiler_params=_cparams("parallel", "arbitrary"),
        name="norm_matmul",
    )(x2, gain3, w_bf16)


IN_TILE_Q, IN_TILE_FORGET, IN_TILE_GATE = 0, 1, 2


def _in_proj_kernel(x_ref, g_ref, w_ref, lb_ref, z_ref, k_ref, h_ref):
    _normalise_rows(x_ref, g_ref, h_ref)
    j = pl.program_id(1)
    half = z_ref.shape[0] // 2

    has_silu = jnp.logical_or(j == IN_TILE_Q, j == IN_TILE_GATE)

    @pl.when(has_silu)
    def _():
        scale = jnp.where(j == IN_TILE_Q, HG_DIM ** -0.5, 1.0)
        for cb0, acc in _sub_tiles(h_ref, w_ref):
            for jj in range(PROJ_SUB // LANES):
                z = acc[:, jj * LANES:(jj + 1) * LANES]
                if cb0 + jj < half:
                    z = z * _sigmoid(z) * scale
                z_ref[cb0 + jj] = z.astype(z_ref.dtype)

    @pl.when(j > IN_TILE_GATE)
    def _():
        for cb0, acc in _sub_tiles(h_ref, w_ref):
            for jj in range(PROJ_SUB // LANES):
                z_ref[cb0 + jj] = acc[:, jj * LANES:(jj + 1) * LANES].astype(z_ref.dtype)

    @pl.when(j == IN_TILE_FORGET)
    def _():
        for cb0, acc in _sub_tiles(h_ref, w_ref):
            for jj in range(PROJ_SUB // LANES):
                z = acc[:, jj * LANES:(jj + 1) * LANES]
                cb = cb0 + jj
                t = jnp.exp(-jnp.abs(z))
                r = 1.0 / (1.0 + t)
                sig_neg = jnp.where(z >= 0.0, t * r, r)
                k_ref[cb] = (1.0 - lb_ref[:, cb * LANES:(cb + 1) * LANES]) * sig_neg


def _in_proj(x2, gain3, w_bf16, lb3, layer, tm, tn):
    m, d = x2.shape
    n_tiles = w_bf16.shape[-1] // tn
    ncb = tn // LANES
    z_tile = lambda j: j - (j >= IN_TILE_FORGET).astype(jnp.int32)
    return pl.pallas_call(
        _in_proj_kernel,
        grid=(m // tm, n_tiles),
        in_specs=[
            pl.BlockSpec((tm, d), lambda i, j: (i, 0)),
            pl.BlockSpec((None, 1, d), lambda i, j: (layer, 0, 0)),
            pl.BlockSpec((None, d, tn), lambda i, j: (layer, 0, j)),
            pl.BlockSpec((None, 1, tn), lambda i, j: (layer, 0, 0)),
        ],
        out_specs=[pl.BlockSpec((ncb, tm, LANES), lambda i, j: (z_tile(j), i, 0)),
                   pl.BlockSpec((ncb, tm, LANES), lambda i, j: (0, i, 0))],
        out_shape=[jax.ShapeDtypeStruct(((n_tiles - 1) * ncb, m, LANES), BF16),
                   jax.ShapeDtypeStruct((ncb, m, LANES), F32)],
        scratch_shapes=[pltpu.VMEM((tm, d), BF16)],
        compiler_params=_cparams("parallel", "arbitrary"),
        name="in_proj",
    )(x2, gain3, w_bf16, lb3)


def _slabs(x):
    return [x[i:i + SUBLANES, :] for i in range(0, x.shape[0], SUBLANES)]


def _join(slabs):
    return jnp.concatenate(slabs, axis=0)


def _double_blocks(qe, kx, tot, b, row8, reverse):
    n = len(qe)
    if b < SUBLANES:
        later = (row8 & b) != 0
        if reverse:
            later = jnp.logical_not(later)
        back, fwd = (SUBLANES - b, b) if reverse else (b, SUBLANES - b)
        prev = [pltpu.roll(t, back, 0) for t in tot]
        nxt = prev if 2 * b == SUBLANES else [pltpu.roll(t, fwd, 0) for t in tot]
        qe = [qe[i] * jnp.where(later, prev[i], 1.0) for i in range(n)]
        kx = [kx[i] * jnp.where(later, 1.0, nxt[i]) for i in range(n)]
        if 2 * b == SUBLANES:
            tot = [tot[i] * prev[i] for i in range(n)]
        else:
            tot = [tot[i] * jnp.where(later, prev[i], nxt[i]) for i in range(n)]
        return qe, kx, tot
    m = b // SUBLANES
    qe, kx, merged = list(qe), list(kx), []
    for p in range(0, n, 2 * m):
        first, second = (p + m, p) if reverse else (p, p + m)
        t_first, t_second = tot[first // m], tot[second // m]
        for i in range(m):
            qe[second + i] = qe[second + i] * t_first
            kx[first + i] = kx[first + i] * t_second
        merged.append(t_first * t_second)
    return qe, kx, merged


def _hgrn2_kernel(q_ref, v_ref, g_ref, kf_ref, kb_ref, gn_ref, o_ref,
                  oi_ref, qf_ref, qb_ref, kvf_ref, kvb_ref, df_ref, db_ref, sf_ref, sb_ref):
    c = HG_CHUNK
    nc = q_ref.shape[0] // c
    nt = (((1,), (1,)), ((), ()))
    tn = (((0,), (0,)), ((), ()))
    row8 = lax.broadcasted_iota(jnp.int32, (SUBLANES, LANES), 0)
    pair_xor = (lax.broadcasted_iota(jnp.int32, (c, c), 0)
                ^ lax.broadcasted_iota(jnp.int32, (c, c), 1))
    n_slab = c // SUBLANES

    def intra(ci, carry):
        rows = pl.ds(pl.multiple_of(ci * c, c), c)
        q = _slabs(q_ref[rows, :].astype(F32))
        v = v_ref[rows, :]
        k_f = _slabs(kf_ref[rows, :])
        k_b = _slabs(kb_ref[rows, :])
        f_f = [jnp.maximum(1.0 - k, F_FLOOR) for k in k_f]
        f_b = [jnp.maximum(1.0 - k, F_FLOOR) for k in k_b]
        fwd = ([q[i] * f_f[i] for i in range(n_slab)], k_f, f_f)
        bwd = ([q[i] * f_b[i] for i in range(n_slab)], k_b, f_b)

        k_both = [k_f[i] + k_b[i] for i in range(n_slab)]
        scores = lax.dot_general(q_ref[rows, :], _join(k_both).astype(BF16), nt,
                                 preferred_element_type=F32)
        b = 1
        while b < c:
            (q_f, x_f, _), (q_b, x_b, _) = fwd, bwd
            if b >= SUBLANES:
                is_right = [(i // (b // SUBLANES)) % 2 == 1 for i in range(n_slab)]
                qh = [q_f[i] if is_right[i] else q_b[i] for i in range(n_slab)]
                kh = [x_b[i] if is_right[i] else x_f[i] for i in range(n_slab)]
            else:
                right = (row8 & b) != 0
                qh = [jnp.where(right, q_f[i], q_b[i]) for i in range(n_slab)]
                kh = [jnp.where(right, x_b[i], x_f[i]) for i in range(n_slab)]
            r = lax.dot_general(_join(qh).astype(BF16), _join(kh).astype(BF16), nt,
                                preferred_element_type=F32)
            scores = jnp.where(pair_xor >= b, r, scores)
            fwd = _double_blocks(*fwd, b, row8, False)
            bwd = _double_blocks(*bwd, b, row8, True)
            b *= 2
        oi_ref[rows, :] = jnp.dot(scores.astype(BF16), v, preferred_element_type=F32)

        (q_f, x_f, tot_f), (q_b, x_b, tot_b) = fwd, bwd
        qf_ref[rows, :] = _join(q_f).astype(BF16)
        qb_ref[rows, :] = _join(q_b).astype(BF16)
        kvf_ref[ci] = lax.dot_general(v, _join(x_f).astype(BF16), tn, preferred_element_type=F32)
        kvb_ref[ci] = lax.dot_general(v, _join(x_b).astype(BF16), tn, preferred_element_type=F32)
        df_ref[ci] = tot_f[0]
        db_ref[ci] = tot_b[0]
        return carry

    lax.fori_loop(0, nc, intra, 0, unroll=8)

    def scan_states(i, carry):
        s_f, s_b = carry
        cb = nc - 1 - i
        sf_ref[i] = s_f.astype(BF16)
        sb_ref[cb] = s_b.astype(BF16)
        s_f = s_f * df_ref[i][0:1, :] + kvf_ref[i]
        s_b = s_b * db_ref[cb][0:1, :] + kvb_ref[cb]
        return s_f, s_b

    zero = jnp.zeros((HG_DIM, HG_DIM), F32)
    lax.fori_loop(0, nc, scan_states, (zero, zero))

    gn = gn_ref[...]

    def finish(ci, carry):
        rows = pl.ds(pl.multiple_of(ci * c, c), c)
        o = oi_ref[rows, :]
        o = o + lax.dot_general(qf_ref[rows, :], sf_ref[ci], nt, preferred_element_type=F32)
        o = o + lax.dot_general(qb_ref[rows, :], sb_ref[ci], nt, preferred_element_type=F32)
        ms = jnp.mean(o * o, axis=-1, keepdims=True)
        o = o * lax.rsqrt(ms + EPS) * gn
        o_ref[rows, :] = (o * g_ref[rows, :].astype(F32)).astype(o_ref.dtype)
        return carry

    lax.fori_loop(0, nc, finish, 0, unroll=8)


def _hgrn2(z3, k3, gnorm3, layer, batch, seq):
    c = HG_CHUNK
    nc = seq // c
    blk = lambda cb0: pl.BlockSpec((None, seq, LANES), lambda b, h: (cb0 + h, b, 0))
    return pl.pallas_call(
        _hgrn2_kernel,
        grid=(batch, HG_HEADS),
        in_specs=[
            blk(CB_HQ), blk(CB_HI), blk(CB_HG),
            blk(0), blk(HG_HEADS),
            pl.BlockSpec((None, 1, LANES), lambda b, h: (layer, 0, 0)),
        ],
        out_specs=pl.BlockSpec((None, seq, LANES), lambda b, h: (h, b, 0)),
        out_shape=jax.ShapeDtypeStruct((HG_HEADS, batch * seq, LANES), BF16),
        scratch_shapes=[
            pltpu.VMEM((seq, LANES), F32),
            pltpu.VMEM((seq, LANES), BF16),
            pltpu.VMEM((seq, LANES), BF16),
            pltpu.VMEM((nc, HG_DIM, HG_DIM), F32),
            pltpu.VMEM((nc, HG_DIM, HG_DIM), F32),
            pltpu.VMEM((nc, SUBLANES, LANES), F32),
            pltpu.VMEM((nc, SUBLANES, LANES), F32),
            pltpu.VMEM((nc, HG_DIM, HG_DIM), BF16),
            pltpu.VMEM((nc, HG_DIM, HG_DIM), BF16),
        ],
        compiler_params=_cparams("parallel", "parallel"),
        name="hgrn2",
    )(z3, z3, z3, k3, k3, gnorm3)


def _natten_kernel(q_ref, k_ref, v_ref, bias_ref, o_ref):
    rows = q_ref.shape[0] // GRID_W
    kh = min(NA_KH, rows)
    win = kh * GRID_W
    nt = (((1,), (1,)), ((), ()))
    lane = lax.broadcasted_iota(jnp.int32, (GRID_W, LANES), 1)
    first = lane < NA_DIM

    def group(gi, carry):
        scored = []
        for u in range(NA_GROUP):
            r = gi * NA_GROUP + u
            rs = jnp.clip(r - kh // 2, 0, rows - kh)
            q2 = q_ref[pl.ds(pl.multiple_of(r * GRID_W, GRID_W), GRID_W), :] * (NA_DIM ** -0.5)
            zero = jnp.zeros_like(q2)
            q_st = jnp.concatenate([jnp.where(first, q2, zero), jnp.where(first, zero, q2)], axis=0)
            kw = k_ref[pl.ds(pl.multiple_of(rs * GRID_W, GRID_W), win), :]
            scored.append((lax.dot_general(q_st, kw, nt, preferred_element_type=F32), r, rs))
        probs = []
        for s, r, rs in scored:
            bm = jnp.concatenate([bias_ref[0, r - rs], bias_ref[1, r - rs]], axis=0)
            s = jnp.where(bm > 0.5 * MASK_NEG, s + bm, MASK_NEG)
            m = jnp.max(s, axis=-1, keepdims=True)
            p = jnp.exp(s - m)
            probs.append((p.astype(BF16), jnp.sum(p, axis=-1, keepdims=True), r, rs))
        for p, l, r, rs in probs:
            vw = v_ref[pl.ds(pl.multiple_of(rs * GRID_W, GRID_W), win), :]
            o_st = jnp.dot(p, vw, preferred_element_type=F32) / l
            o = jnp.where(first, o_st[:GRID_W], o_st[GRID_W:])
            o_ref[pl.ds(pl.multiple_of(r * GRID_W, GRID_W), GRID_W), :] = o.astype(o_ref.dtype)
        return carry

    lax.fori_loop(0, rows // NA_GROUP, group, 0)


def _natten_bias_table(rpb, rows):
    kh = min(NA_KH, rows)
    r = np.arange(rows)
    rs = np.clip(r - kh // 2, 0, rows - kh)
    n_case = int((r - rs).max()) + 1
    case = np.arange(n_case)
    dr = np.arange(kh)[None, :] - case[:, None] + (NA_KH - 1)
    c = np.arange(GRID_W)
    col_start = np.clip(c - NA_KW // 2, 0, GRID_W - NA_KW)
    col_mask = (c[None, :] >= col_start[:, None]) & (c[None, :] < col_start[:, None] + NA_KW)
    dc = np.clip(c[None, :] - c[:, None], -(NA_KW - 1), NA_KW - 1) + (NA_KW - 1)
    depth, heads, _, n_dc = rpb.shape
    by_row = jnp.take(rpb.astype(F32).reshape(depth * heads, -1, n_dc), dr.reshape(-1), axis=1)
    by_row = by_row.reshape(depth * heads, n_case, kh, n_dc)
    pick = ((dc[None, :, :] == np.arange(n_dc)[:, None, None]) & col_mask[None]).astype(np.float32)
    masked = np.where(col_mask, 0.0, MASK_NEG).astype(np.float32)
    tab = jnp.einsum("xcjd,dqk->xcqjk", by_row, pick, precision=lax.Precision.HIGHEST)
    tab = tab + masked[None, None, :, None, :]
    return tab.reshape(depth, heads, n_case, GRID_W, kh * GRID_W)


def _natten(z3, bias_tab, layer, batch, seq):
    n_case, win = bias_tab.shape[2], bias_tab.shape[4]
    blk = lambda cb0: pl.BlockSpec((None, seq, LANES), lambda b, p: (cb0 + p, b, 0))
    return pl.pallas_call(
        _natten_kernel,
        grid=(batch, NA_HEADS // 2),
        in_specs=[
            blk(CB_NQ), blk(CB_NK), blk(CB_NV),
            pl.BlockSpec((None, 2, n_case, GRID_W, win), lambda b, p: (layer, p, 0, 0, 0)),
        ],
        out_specs=pl.BlockSpec((None, seq, LANES), lambda b, p: (p, b, 0)),
        out_shape=jax.ShapeDtypeStruct((NA_HEADS // 2, batch * seq, LANES), BF16),
        compiler_params=_cparams("parallel", "parallel"),
        name="natten",
    )(z3, z3, z3, bias_tab)


def _memattn_kernel(q_ref, kv_ref, o_ref):
    nt = (((1,), (1,)), ((), ()))
    for h in range(CA_HEADS):
        s = lax.dot_general(q_ref[h], kv_ref[h], nt, preferred_element_type=F32) * (CA_DIM ** -0.5)
        m = jnp.max(s, axis=-1, keepdims=True)
        p = jnp.exp(s - m)
        l = jnp.sum(p, axis=-1, keepdims=True)
        o = jnp.dot(p.astype(BF16), kv_ref[CA_HEADS + h], preferred_element_type=F32) / l
        o_ref[h] = o.astype(o_ref.dtype)


def _memattn(z3, kv3, batch, seq, mem_len, tq):
    nq = seq // tq
    return pl.pallas_call(
        _memattn_kernel,
        grid=(batch, nq),
        in_specs=[
            pl.BlockSpec((CA_HEADS, tq, LANES), lambda b, i: (CB_CQ // CA_HEADS, b * nq + i, 0)),
            pl.BlockSpec((2 * CA_HEADS, mem_len, LANES), lambda b, i: (0, b, 0)),
        ],
        out_specs=pl.BlockSpec((CA_HEADS, tq, LANES), lambda b, i: (0, b * nq + i, 0)),
        out_shape=jax.ShapeDtypeStruct((CA_HEADS, batch * seq, LANES), BF16),
        compiler_params=_cparams("parallel", "parallel"),
        name="memattn",
    )(z3, kv3)


def _cat(ref):
    return jnp.concatenate([ref[j] for j in range(ref.shape[0])], axis=-1)


def _merge_kernel(x_ref, ohg_ref, ona_ref, oca_ref, ghg_ref, gna_ref, gca_ref,
                  whg_ref, wna_ref, wca_ref, wout_ref, o_ref):
    def branch(o3_ref, w_ref, g_ref):
        y = jnp.dot(_cat(o3_ref), w_ref[...], preferred_element_type=F32)
        return _sigmoid(_cat(g_ref).astype(F32)) * y

    merged = branch(ohg_ref, whg_ref, ghg_ref)
    merged = merged + branch(ona_ref, wna_ref, gna_ref)
    merged = merged + branch(oca_ref, wca_ref, gca_ref)
    o_ref[...] = x_ref[...] + jnp.dot(merged.astype(BF16), wout_ref[...],
                                      preferred_element_type=F32)


def _merge(x2, ohg3, ona3, oca3, z3, w_hg_o, w_na_o, w_ca_o, w_out, layer, tm):
    m, d = x2.shape
    ncb = d // LANES
    act = lambda n: pl.BlockSpec((n, tm, LANES), lambda i: (0, i, 0))
    gate = lambda cb0: pl.BlockSpec((ncb, tm, LANES), lambda i: (cb0 // ncb, i, 0))
    full = lambda a: pl.BlockSpec((None,) + a.shape[1:], lambda i: (layer, 0, 0))
    return pl.pallas_call(
        _merge_kernel,
        grid=(m // tm,),
        in_specs=[
            pl.BlockSpec((tm, d), lambda i: (i, 0)),
            act(ohg3.shape[0]), act(ona3.shape[0]), act(oca3.shape[0]),
            gate(CB_GHG), gate(CB_GNA), gate(CB_GCA),
            full(w_hg_o), full(w_na_o), full(w_ca_o), full(w_out),
        ],
        out_specs=pl.BlockSpec((tm, d), lambda i: (i, 0)),
        out_shape=jax.ShapeDtypeStruct((m, d), F32),
        compiler_params=_cparams("parallel"),
        name="merge",
    )(x2, ohg3, ona3, oca3, z3, z3, z3, w_hg_o, w_na_o, w_ca_o, w_out)


def _ffn_kernel(x_ref, xp_ref, xn_ref, gain_ref, wup_ref, cw_ref, cb_ref, wd_ref, fin_ref, o_ref,
                h_ref, y_ref, *, tiles_per_seq, final_norm, ts):
    i = pl.program_id(0)
    tm = x_ref.shape[0]
    halo = SUBLANES

    def normed(x):
        ms = jnp.mean(x * x, axis=-1, keepdims=True)
        return x * lax.rsqrt(ms + EPS) * gain_ref[...]

    keep_prev = (i % tiles_per_seq != 0).astype(F32)
    keep_next = (i % tiles_per_seq != tiles_per_seq - 1).astype(F32)
    h_ref[0:halo, :] = (normed(xp_ref[...]) * keep_prev).astype(BF16)
    h_ref[halo:halo + tm, :] = normed(x_ref[...]).astype(BF16)
    h_ref[halo + tm:, :] = (normed(xn_ref[...]) * keep_next).astype(BF16)

    def conv_proj(cols):
        u = jnp.dot(h_ref[...], wup_ref[:, cols], preferred_element_type=F32)
        rows = u.shape[0]
        prev = pltpu.roll(u, 1, 0)[halo:halo + tm]
        nxt = pltpu.roll(u, rows - 1, 0)[halo:halo + tm]
        out = prev * cw_ref[0:1, cols]
        out = out + u[halo:halo + tm] * cw_ref[1:2, cols]
        out = out + nxt * cw_ref[2:3, cols]
        return out + cb_ref[:, cols]

    def sub_tile(s, carry):
        cols_a = pl.ds(pl.multiple_of(s * ts, ts), ts)
        cols_g = pl.ds(pl.multiple_of(D_FF + s * ts, ts), ts)
        a = conv_proj(cols_a)
        g = conv_proj(cols_g)
        c = np.float32(np.sqrt(2 / np.pi))
        th = jnp.tanh(a * ((a * a) * (c * np.float32(0.044715)) + c))
        half_ag = (a * g) * 0.5
        y_ref[:, cols_a] = (half_ag + half_ag * th).astype(BF16)
        return carry

    lax.fori_loop(0, D_FF // ts, sub_tile, 0, unroll=True)

    out = x_ref[...] + jnp.dot(y_ref[...], wd_ref[...], preferred_element_type=F32)
    if final_norm:
        ms = jnp.mean(out * out, axis=-1, keepdims=True)
        out = out * lax.rsqrt(ms + EPS) * fin_ref[...]
    o_ref[...] = out


def _conv_ffn(x2, gain3, w_up, conv_w, conv_b3, w_down, fin_gain, layer, seq, tm, ts, final_norm):
    m, d = x2.shape
    hb = tm // SUBLANES
    n_halo = m // SUBLANES
    kern = functools.partial(_ffn_kernel, tiles_per_seq=seq // tm, final_norm=final_norm, ts=ts)
    resident = lambda a: pl.BlockSpec((None,) + a.shape[1:], lambda i: (layer, 0, 0),
                                      pipeline_mode=pl.Buffered(1))
    return pl.pallas_call(
        kern,
        grid=(m // tm,),
        in_specs=[
            pl.BlockSpec((tm, d), lambda i: (i, 0)),
            pl.BlockSpec((SUBLANES, d), lambda i: (jnp.maximum(i * hb - 1, 0), 0)),
            pl.BlockSpec((SUBLANES, d), lambda i: (jnp.minimum((i + 1) * hb, n_halo - 1), 0)),
            pl.BlockSpec((None, 1, d), lambda i: (layer, 0, 0)),
            resident(w_up), resident(conv_w), resident(conv_b3), resident(w_down),
            pl.BlockSpec((1, d), lambda i: (0, 0)),
        ],
        out_specs=pl.BlockSpec((tm, d), lambda i: (i, 0)),
        out_shape=jax.ShapeDtypeStruct((m, d), F32),
        scratch_shapes=[
            pltpu.VMEM((tm + 2 * SUBLANES, d), BF16),
            pltpu.VMEM((tm, D_FF), BF16),
        ],
        compiler_params=_cparams("parallel"),
        name="conv_ffn",
    )(x2, x2, x2, gain3, w_up, conv_w, conv_b3, w_down, fin_gain.reshape(1, d))


def kernel(x, mem, norm_mix, w_in, hg_lb_logits, hg_gnorm, w_hg_o, na_rpb, w_na_o, mem_norm,
           w_mem_kv, w_ca_o, w_out, norm_ffn, w_up, conv_w, conv_b, w_down, norm_final):
    batch, seq, d = x.shape
    mem_len = mem.shape[1]
    depth = w_in.shape[0]
    assert d == D_MODEL and seq % GRID_W == 0 and seq % HG_CHUNK == 0
    hgw = HG_HEADS * HG_DIM

    p_lb = jax.nn.softmax(hg_lb_logits.astype(F32), axis=0)
    lower_bounds = jnp.clip(jnp.cumsum(p_lb, axis=0) - p_lb[0], 0.0, 1.0)

    bf = lambda a: a.astype(BF16)
    w_in, w_mem_kv, w_hg_o, w_na_o, w_ca_o, w_out, w_up, w_down = map(
        bf, (w_in, w_mem_kv, w_hg_o, w_na_o, w_ca_o, w_out, w_up, w_down))
    vec3 = lambda a: a.reshape(depth, 1, -1)
    norm_mix3, norm_ffn3, gnorm3, conv_b3 = map(vec3, (norm_mix, norm_ffn, hg_gnorm, conv_b))
    lb3 = vec3(lower_bounds)
    mem_norm3 = jnp.broadcast_to(mem_norm.reshape(1, 1, d), (depth, 1, d))
    bias_tab = _natten_bias_table(na_rpb, seq // GRID_W)

    assert w_in.shape[-1] == (N_CB + 2 * hgw // LANES) * LANES

    x2 = x.reshape(batch * seq, d)
    mem2 = mem.reshape(batch * mem_len, d)

    for l in range(depth):
        z3, k3 = _in_proj(x2, norm_mix3, w_in, lb3, l, tm=1024, tn=2 * hgw)
        kv3 = _norm_matmul(mem2, mem_norm3, w_mem_kv, l, tm=batch * mem_len,
                           tn=w_mem_kv.shape[-1])

        ohg3 = _hgrn2(z3, k3, gnorm3, l, batch, seq)
        ona3 = _natten(z3, bias_tab, l, batch, seq)
        oca3 = _memattn(z3, kv3, batch, seq, mem_len, tq=512)

        x2 = _merge(x2, ohg3, ona3, oca3, z3, w_hg_o, w_na_o, w_ca_o, w_out, l, tm=512)
        x2 = _conv_ffn(x2, norm_ffn3, w_up, conv_w, conv_b3, w_down, norm_final, l, seq,
                       tm=512, ts=256, final_norm=(l == depth - 1))
    return x2.reshape(batch, seq, d)
```

```python
import functools

import numpy as np
import jax
import jax.numpy as jnp
from jax import lax
from jax.experimental import pallas as pl
from jax.experimental.pallas import tpu as pltpu

D_MODEL = 1024
GRID_W = 64
HG_HEADS = 8
HG_DIM = 128
NA_HEADS = 8
NA_DIM = 64
NA_KH = 8
NA_KW = 16
CA_HEADS = 4
CA_DIM = 128
D_FF = 2816
CONV_W = 3
EPS = 1e-6
F_FLOOR = 1e-12
MASK_NEG = -1e30

LANES = 128
SUBLANES = 8
VMEM_LIMIT = 56 * 1024 * 1024

HG_CHUNK = 128
HG_UNROLL = 8
NA_GROUP = 8

CB_HQ, CB_HI, CB_HG = 0, 8, 16
CB_NQ, CB_NK, CB_NV = 24, 28, 32
CB_CQ = 36
CB_GHG, CB_GNA, CB_GCA = 40, 48, 56
N_CB = 64

F32 = jnp.float32
BF16 = jnp.bfloat16


def _cparams(*sem):
    return pltpu.CompilerParams(dimension_semantics=sem, vmem_limit_bytes=VMEM_LIMIT)


def _sigmoid(x):
    return 1.0 / (1.0 + jnp.exp(-x))


PROJ_SUB = 512


def _normalise_rows(x_ref, g_ref, h_ref):
    @pl.when(pl.program_id(1) == 0)
    def _():
        x = x_ref[...]
        ms = jnp.mean(x * x, axis=-1, keepdims=True)
        h_ref[...] = (x * lax.rsqrt(ms + EPS) * g_ref[...]).astype(BF16)


def _sub_tiles(h_ref, w_ref):
    for s in range(w_ref.shape[1] // PROJ_SUB):
        acc = jnp.dot(h_ref[...], w_ref[:, s * PROJ_SUB:(s + 1) * PROJ_SUB],
                      preferred_element_type=F32)
        yield s * (PROJ_SUB // LANES), acc


def _norm_mm_kernel(x_ref, g_ref, w_ref, o_ref, h_ref):
    _normalise_rows(x_ref, g_ref, h_ref)
    for cb0, acc in _sub_tiles(h_ref, w_ref):
        for jj in range(PROJ_SUB // LANES):
            o_ref[cb0 + jj] = acc[:, jj * LANES:(jj + 1) * LANES].astype(o_ref.dtype)


def _norm_matmul(x2, gain3, w_bf16, layer, tm, tn):
    m, d = x2.shape
    n = w_bf16.shape[-1]
    return pl.pallas_call(
        _norm_mm_kernel,
        grid=(m // tm, n // tn),
        in_specs=[
            pl.BlockSpec((tm, d), lambda i, j: (i, 0)),
            pl.BlockSpec((None, 1, d), lambda i, j: (layer, 0, 0)),
            pl.BlockSpec((None, d, tn), lambda i, j: (layer, 0, j)),
        ],
        out_specs=pl.BlockSpec((tn // LANES, tm, LANES), lambda i, j: (j, i, 0)),
        out_shape=jax.ShapeDtypeStruct((n // LANES, m, LANES), BF16),
        scratch_shapes=[pltpu.VMEM((tm, d), BF16)],
        compiler_params=_cparams("parallel", "arbitrary"),
        name="norm_matmul",
    )(x2, gain3, w_bf16)


IN_TILE_Q, IN_TILE_FORGET, IN_TILE_GATE = 0, 1, 2


def _in_proj_kernel(x_ref, g_ref, w_ref, lb_ref, z_ref, k_ref, h_ref):
    _normalise_rows(x_ref, g_ref, h_ref)
    j = pl.program_id(1)
    half = z_ref.shape[0] // 2

    has_silu = jnp.logical_or(j == IN_TILE_Q, j == IN_TILE_GATE)

    @pl.when(has_silu)
    def _():
        scale = jnp.where(j == IN_TILE_Q, HG_DIM ** -0.5, 1.0)
        for cb0, acc in _sub_tiles(h_ref, w_ref):
            for jj in range(PROJ_SUB // LANES):
                z = acc[:, jj * LANES:(jj + 1) * LANES]
                if cb0 + jj < half:
                    z = z * _sigmoid(z) * scale
                z_ref[cb0 + jj] = z.astype(z_ref.dtype)

    @pl.when(j > IN_TILE_GATE)
    def _():
        for cb0, acc in _sub_tiles(h_ref, w_ref):
            for jj in range(PROJ_SUB // LANES):
                z_ref[cb0 + jj] = acc[:, jj * LANES:(jj + 1) * LANES].astype(z_ref.dtype)

    @pl.when(j == IN_TILE_FORGET)
    def _():
        for cb0, acc in _sub_tiles(h_ref, w_ref):
            for jj in range(PROJ_SUB // LANES):
                z = acc[:, jj * LANES:(jj + 1) * LANES]
                cb = cb0 + jj
                t = jnp.exp(-jnp.abs(z))
                r = 1.0 / (1.0 + t)
                sig_neg = jnp.where(z >= 0.0, t * r, r)
                k_ref[cb] = (1.0 - lb_ref[:, cb * LANES:(cb + 1) * LANES]) * sig_neg


def _in_proj(x2, gain3, w_bf16, lb3, layer, tm, tn):
    m, d = x2.shape
    n_tiles = w_bf16.shape[-1] // tn
    ncb = tn // LANES
    z_tile = lambda j: j - (j >= IN_TILE_FORGET).astype(jnp.int32)
    return pl.pallas_call(
        _in_proj_kernel,
        grid=(m // tm, n_tiles),
        in_specs=[
            pl.BlockSpec((tm, d), lambda i, j: (i, 0)),
            pl.BlockSpec((None, 1, d), lambda i, j: (layer, 0, 0)),
            pl.BlockSpec((None, d, tn), lambda i, j: (layer, 0, j)),
            pl.BlockSpec((None, 1, tn), lambda i, j: (layer, 0, 0)),
        ],
        out_specs=[pl.BlockSpec((ncb, tm, LANES), lambda i, j: (z_tile(j), i, 0)),
                   pl.BlockSpec((ncb, tm, LANES), lambda i, j: (0, i, 0))],
        out_shape=[jax.ShapeDtypeStruct(((n_tiles - 1) * ncb, m, LANES), BF16),
                   jax.ShapeDtypeStruct((ncb, m, LANES), F32)],
        scratch_shapes=[pltpu.VMEM((tm, d), BF16)],
        compiler_params=_cparams("parallel", "arbitrary"),
        name="in_proj",
    )(x2, gain3, w_bf16, lb3)


def _slabs(x):
    return [x[i:i + SUBLANES, :] for i in range(0, x.shape[0], SUBLANES)]


def _join(slabs):
    return jnp.concatenate(slabs, axis=0)


def _pair_scores(q_slabs, k_slabs):
    k_t = jnp.transpose(_join(k_slabs)).astype(BF16)
    return jnp.dot(_join(q_slabs).astype(BF16), k_t, preferred_element_type=F32)


def _double_blocks(qe, kx, tot, b, row8, reverse):
    n = len(qe)
    if b < SUBLANES:
        later = (row8 & b) != 0
        if reverse:
            later = jnp.logical_not(later)
        back, fwd = (SUBLANES - b, b) if reverse else (b, SUBLANES - b)
        prev = [pltpu.roll(t, back, 0) for t in tot]
        nxt = prev if 2 * b == SUBLANES else [pltpu.roll(t, fwd, 0) for t in tot]
        qe = [qe[i] * jnp.where(later, prev[i], 1.0) for i in range(n)]
        kx = [kx[i] * jnp.where(later, 1.0, nxt[i]) for i in range(n)]
        if 2 * b == SUBLANES:
            tot = [tot[i] * prev[i] for i in range(n)]
        else:
            tot = [tot[i] * jnp.where(later, prev[i], nxt[i]) for i in range(n)]
        return qe, kx, tot
    m = b // SUBLANES
    qe, kx, merged = list(qe), list(kx), []
    for p in range(0, n, 2 * m):
        first, second = (p + m, p) if reverse else (p, p + m)
        t_first, t_second = tot[first // m], tot[second // m]
        for i in range(m):
            qe[second + i] = qe[second + i] * t_first
            kx[first + i] = kx[first + i] * t_second
        merged.append(t_first * t_second)
    return qe, kx, merged


def _hgrn2_kernel(q_ref, v_ref, g_ref, kf_ref, kb_ref, gn_ref, o_ref,
                  oi_ref, qf_ref, qb_ref, kvf_ref, kvb_ref, df_ref, db_ref, sf_ref, sb_ref):
    c = HG_CHUNK
    nc = q_ref.shape[0] // c
    nt = (((1,), (1,)), ((), ()))
    tn = (((0,), (0,)), ((), ()))
    row8 = lax.broadcasted_iota(jnp.int32, (SUBLANES, LANES), 0)
    pair_xor = (lax.broadcasted_iota(jnp.int32, (c, c), 0)
                ^ lax.broadcasted_iota(jnp.int32, (c, c), 1))
    n_slab = c // SUBLANES

    def intra(ci, carry):
        rows = pl.ds(pl.multiple_of(ci * c, c), c)
        q = _slabs(q_ref[rows, :].astype(F32))
        v = v_ref[rows, :]
        k_f = _slabs(kf_ref[rows, :])
        k_b = _slabs(kb_ref[rows, :])
        f_f = [jnp.maximum(1.0 - k, F_FLOOR) for k in k_f]
        f_b = [jnp.maximum(1.0 - k, F_FLOOR) for k in k_b]
        fwd = ([q[i] * f_f[i] for i in range(n_slab)], k_f, f_f)
        bwd = ([q[i] * f_b[i] for i in range(n_slab)], k_b, f_b)

        k_both = [k_f[i] + k_b[i] for i in range(n_slab)]
        scores = _pair_scores(q, k_both)
        b = 1
        while b < c:
            (q_f, x_f, _), (q_b, x_b, _) = fwd, bwd
            if b >= SUBLANES:
                is_right = [(i // (b // SUBLANES)) % 2 == 1 for i in range(n_slab)]
                qh = [q_f[i] if is_right[i] else q_b[i] for i in range(n_slab)]
                kh = [x_b[i] if is_right[i] else x_f[i] for i in range(n_slab)]
            else:
                right = (row8 & b) != 0
                qh = [jnp.where(right, q_f[i], q_b[i]) for i in range(n_slab)]
                kh = [jnp.where(right, x_b[i], x_f[i]) for i in range(n_slab)]
            r = _pair_scores(qh, kh)
            scores = jnp.where(pair_xor >= b, r, scores)
            fwd = _double_blocks(*fwd, b, row8, False)
            bwd = _double_blocks(*bwd, b, row8, True)
            b *= 2
        oi_ref[rows, :] = jnp.dot(scores.astype(BF16), v, preferred_element_type=F32)

        (q_f, x_f, tot_f), (q_b, x_b, tot_b) = fwd, bwd
        qf_ref[rows, :] = _join(q_f).astype(BF16)
        qb_ref[rows, :] = _join(q_b).astype(BF16)
        kvf_ref[ci] = lax.dot_general(v, _join(x_f).astype(BF16), tn, preferred_element_type=F32)
        kvb_ref[ci] = lax.dot_general(v, _join(x_b).astype(BF16), tn, preferred_element_type=F32)
        df_ref[ci] = tot_f[0]
        db_ref[ci] = tot_b[0]
        return carry

    lax.fori_loop(0, nc, intra, 0, unroll=HG_UNROLL)

    def scan_states(i, carry):
        s_f, s_b = carry
        cb = nc - 1 - i
        sf_ref[i] = s_f.astype(BF16)
        sb_ref[cb] = s_b.astype(BF16)
        s_f = s_f * df_ref[i][0:1, :] + kvf_ref[i]
        s_b = s_b * db_ref[cb][0:1, :] + kvb_ref[cb]
        return s_f, s_b

    zero = jnp.zeros((HG_DIM, HG_DIM), F32)
    lax.fori_loop(0, nc, scan_states, (zero, zero))

    gn = gn_ref[...]

    def finish(ci, carry):
        rows = pl.ds(pl.multiple_of(ci * c, c), c)
        o = oi_ref[rows, :]
        o = o + lax.dot_general(qf_ref[rows, :], sf_ref[ci], nt, preferred_element_type=F32)
        o = o + lax.dot_general(qb_ref[rows, :], sb_ref[ci], nt, preferred_element_type=F32)
        ms = jnp.mean(o * o, axis=-1, keepdims=True)
        o = o * lax.rsqrt(ms + EPS) * gn
        o_ref[rows, :] = (o * g_ref[rows, :].astype(F32)).astype(o_ref.dtype)
        return carry

    lax.fori_loop(0, nc, finish, 0, unroll=HG_UNROLL)


def _hgrn2(z3, k3, gnorm3, layer, batch, seq):
    c = HG_CHUNK
    nc = seq // c
    blk = lambda cb0: pl.BlockSpec((None, seq, LANES), lambda b, h: (cb0 + h, b, 0))
    return pl.pallas_call(
        _hgrn2_kernel,
        grid=(batch, HG_HEADS),
        in_specs=[
            blk(CB_HQ), blk(CB_HI), blk(CB_HG),
            blk(0), blk(HG_HEADS),
            pl.BlockSpec((None, 1, LANES), lambda b, h: (layer, 0, 0)),
        ],
        out_specs=pl.BlockSpec((None, seq, LANES), lambda b, h: (h, b, 0)),
        out_shape=jax.ShapeDtypeStruct((HG_HEADS, batch * seq, LANES), BF16),
        scratch_shapes=[
            pltpu.VMEM((seq, LANES), F32),
            pltpu.VMEM((seq, LANES), BF16),
            pltpu.VMEM((seq, LANES), BF16),
            pltpu.VMEM((nc, HG_DIM, HG_DIM), F32),
            pltpu.VMEM((nc, HG_DIM, HG_DIM), F32),
            pltpu.VMEM((nc, SUBLANES, LANES), F32),
            pltpu.VMEM((nc, SUBLANES, LANES), F32),
            pltpu.VMEM((nc, HG_DIM, HG_DIM), BF16),
            pltpu.VMEM((nc, HG_DIM, HG_DIM), BF16),
        ],
        compiler_params=_cparams("parallel", "parallel"),
        name="hgrn2",
    )(z3, z3, z3, k3, k3, gnorm3)


def _natten_kernel(q_ref, k_ref, v_ref, bias_ref, o_ref):
    rows = q_ref.shape[0] // GRID_W
    kh = min(NA_KH, rows)
    win = kh * GRID_W
    nt = (((1,), (1,)), ((), ()))
    lane = lax.broadcasted_iota(jnp.int32, (GRID_W, LANES), 1)
    first = lane < NA_DIM

    def group(gi, carry):
        scored = []
        for u in range(NA_GROUP):
            r = gi * NA_GROUP + u
            rs = jnp.clip(r - kh // 2, 0, rows - kh)
            q2 = q_ref[pl.ds(pl.multiple_of(r * GRID_W, GRID_W), GRID_W), :] * (NA_DIM ** -0.5)
            zero = jnp.zeros_like(q2)
            q_st = jnp.concatenate([jnp.where(first, q2, zero), jnp.where(first, zero, q2)], axis=0)
            kw = k_ref[pl.ds(pl.multiple_of(rs * GRID_W, GRID_W), win), :]
            scored.append((lax.dot_general(q_st, kw, nt, preferred_element_type=F32), r, rs))
        probs = []
        for s, r, rs in scored:
            bm = jnp.concatenate([bias_ref[0, r - rs], bias_ref[1, r - rs]], axis=0)
            s = jnp.where(bm > 0.5 * MASK_NEG, s + bm, MASK_NEG)
            m = jnp.max(s, axis=-1, keepdims=True)
            p = jnp.exp(s - m)
            probs.append((p.astype(BF16), jnp.sum(p, axis=-1, keepdims=True), r, rs))
        for p, l, r, rs in probs:
            vw = v_ref[pl.ds(pl.multiple_of(rs * GRID_W, GRID_W), win), :]
            o_st = jnp.dot(p, vw, preferred_element_type=F32) / l
            o = jnp.where(first, o_st[:GRID_W], o_st[GRID_W:])
            o_ref[pl.ds(pl.multiple_of(r * GRID_W, GRID_W), GRID_W), :] = o.astype(o_ref.dtype)
        return carry

    lax.fori_loop(0, rows // NA_GROUP, group, 0)


def _natten_bias_table(rpb, rows):
    kh = min(NA_KH, rows)
    r = np.arange(rows)
    rs = np.clip(r - kh // 2, 0, rows - kh)
    n_case = int((r - rs).max()) + 1
    case = np.arange(n_case)
    dr = np.arange(kh)[None, :] - case[:, None] + (NA_KH - 1)
    c = np.arange(GRID_W)
    col_start = np.clip(c - NA_KW // 2, 0, GRID_W - NA_KW)
    col_mask = (c[None, :] >= col_start[:, None]) & (c[None, :] < col_start[:, None] + NA_KW)
    dc = np.clip(c[None, :] - c[:, None], -(NA_KW - 1), NA_KW - 1) + (NA_KW - 1)
    depth, heads, _, n_dc = rpb.shape
    by_row = jnp.take(rpb.astype(F32).reshape(depth * heads, -1, n_dc), dr.reshape(-1), axis=1)
    by_row = by_row.reshape(depth * heads, n_case, kh, n_dc)
    pick = ((dc[None, :, :] == np.arange(n_dc)[:, None, None]) & col_mask[None]).astype(np.float32)
    masked = np.where(col_mask, 0.0, MASK_NEG).astype(np.float32)
    tab = jnp.einsum("xcjd,dqk->xcqjk", by_row, pick, precision=lax.Precision.HIGHEST)
    tab = tab + masked[None, None, :, None, :]
    return tab.reshape(depth, heads, n_case, GRID_W, kh * GRID_W)


def _natten(z3, bias_tab, layer, batch, seq):
    n_case, win = bias_tab.shape[2], bias_tab.shape[4]
    blk = lambda cb0: pl.BlockSpec((None, seq, LANES), lambda b, p: (cb0 + p, b, 0))
    return pl.pallas_call(
        _natten_kernel,
        grid=(batch, NA_HEADS // 2),
        in_specs=[
            blk(CB_NQ), blk(CB_NK), blk(CB_NV),
            pl.BlockSpec((None, 2, n_case, GRID_W, win), lambda b, p: (layer, p, 0, 0, 0)),
        ],
        out_specs=pl.BlockSpec((None, seq, LANES), lambda b, p: (p, b, 0)),
        out_shape=jax.ShapeDtypeStruct((NA_HEADS // 2, batch * seq, LANES), BF16),
        compiler_params=_cparams("parallel", "parallel"),
        name="natten",
    )(z3, z3, z3, bias_tab)


def _memattn_kernel(q_ref, kv_ref, o_ref):
    nt = (((1,), (1,)), ((), ()))
    for h in range(CA_HEADS):
        s = lax.dot_general(q_ref[h], kv_ref[h], nt, preferred_element_type=F32) * (CA_DIM ** -0.5)
        m = jnp.max(s, axis=-1, keepdims=True)
        p = jnp.exp(s - m)
        l = jnp.sum(p, axis=-1, keepdims=True)
        o = jnp.dot(p.astype(BF16), kv_ref[CA_HEADS + h], preferred_element_type=F32) / l
        o_ref[h] = o.astype(o_ref.dtype)


def _memattn(z3, kv3, batch, seq, mem_len, tq):
    nq = seq // tq
    return pl.pallas_call(
        _memattn_kernel,
        grid=(batch, nq),
        in_specs=[
            pl.BlockSpec((CA_HEADS, tq, LANES), lambda b, i: (CB_CQ // CA_HEADS, b * nq + i, 0)),
            pl.BlockSpec((2 * CA_HEADS, mem_len, LANES), lambda b, i: (0, b, 0)),
        ],
        out_specs=pl.BlockSpec((CA_HEADS, tq, LANES), lambda b, i: (0, b * nq + i, 0)),
        out_shape=jax.ShapeDtypeStruct((CA_HEADS, batch * seq, LANES), BF16),
        compiler_params=_cparams("parallel", "parallel"),
        name="memattn",
    )(z3, kv3)


def _cat(ref):
    return jnp.concatenate([ref[j] for j in range(ref.shape[0])], axis=-1)


def _merge_kernel(x_ref, ohg_ref, ona_ref, oca_ref, ghg_ref, gna_ref, gca_ref,
                  whg_ref, wna_ref, wca_ref, wout_ref, o_ref):
    def branch(o3_ref, w_ref, g_ref):
        y = jnp.dot(_cat(o3_ref), w_ref[...], preferred_element_type=F32)
        return _sigmoid(_cat(g_ref).astype(F32)) * y

    merged = branch(ohg_ref, whg_ref, ghg_ref)
    merged = merged + branch(ona_ref, wna_ref, gna_ref)
    merged = merged + branch(oca_ref, wca_ref, gca_ref)
    o_ref[...] = x_ref[...] + jnp.dot(merged.astype(BF16), wout_ref[...],
                                      preferred_element_type=F32)


def _merge(x2, ohg3, ona3, oca3, z3, w_hg_o, w_na_o, w_ca_o, w_out, layer, tm):
    m, d = x2.shape
    ncb = d // LANES
    act = lambda n: pl.BlockSpec((n, tm, LANES), lambda i: (0, i, 0))
    gate = lambda cb0: pl.BlockSpec((ncb, tm, LANES), lambda i: (cb0 // ncb, i, 0))
    full = lambda a: pl.BlockSpec((None,) + a.shape[1:], lambda i: (layer, 0, 0))
    return pl.pallas_call(
        _merge_kernel,
        grid=(m // tm,),
        in_specs=[
            pl.BlockSpec((tm, d), lambda i: (i, 0)),
            act(ohg3.shape[0]), act(ona3.shape[0]), act(oca3.shape[0]),
            gate(CB_GHG), gate(CB_GNA), gate(CB_GCA),
            full(w_hg_o), full(w_na_o), full(w_ca_o), full(w_out),
        ],
        out_specs=pl.BlockSpec((tm, d), lambda i: (i, 0)),
        out_shape=jax.ShapeDtypeStruct((m, d), F32),
        compiler_params=_cparams("parallel"),
        name="merge",
    )(x2, ohg3, ona3, oca3, z3, z3, z3, w_hg_o, w_na_o, w_ca_o, w_out)


def _ffn_kernel(x_ref, xp_ref, xn_ref, gain_ref, wup_ref, cw_ref, cb_ref, wd_ref, fin_ref, o_ref,
                h_ref, y_ref, *, tiles_per_seq, final_norm, ts):
    i = pl.program_id(0)
    tm = x_ref.shape[0]
    halo = SUBLANES

    def normed(x):
        ms = jnp.mean(x * x, axis=-1, keepdims=True)
        return x * lax.rsqrt(ms + EPS) * gain_ref[...]

    keep_prev = (i % tiles_per_seq != 0).astype(F32)
    keep_next = (i % tiles_per_seq != tiles_per_seq - 1).astype(F32)
    h_ref[0:halo, :] = (normed(xp_ref[...]) * keep_prev).astype(BF16)
    h_ref[halo:halo + tm, :] = normed(x_ref[...]).astype(BF16)
    h_ref[halo + tm:, :] = (normed(xn_ref[...]) * keep_next).astype(BF16)

    def conv_proj(cols):
        u = jnp.dot(h_ref[...], wup_ref[:, cols], preferred_element_type=F32)
        rows = u.shape[0]
        prev = pltpu.roll(u, 1, 0)[halo:halo + tm]
        nxt = pltpu.roll(u, rows - 1, 0)[halo:halo + tm]
        out = prev * cw_ref[0:1, cols]
        out = out + u[halo:halo + tm] * cw_ref[1:2, cols]
        out = out + nxt * cw_ref[2:3, cols]
        return out + cb_ref[:, cols]

    def sub_tile(s, carry):
        cols_a = pl.ds(pl.multiple_of(s * ts, ts), ts)
        cols_g = pl.ds(pl.multiple_of(D_FF + s * ts, ts), ts)
        a = conv_proj(cols_a)
        g = conv_proj(cols_g)
        c = np.float32(np.sqrt(2 / np.pi))
        th = jnp.tanh(a * ((a * a) * (c * np.float32(0.044715)) + c))
        half_ag = (a * g) * 0.5
        y_ref[:, cols_a] = (half_ag + half_ag * th).astype(BF16)
        return carry

    lax.fori_loop(0, D_FF // ts, sub_tile, 0, unroll=True)

    out = x_ref[...] + jnp.dot(y_ref[...], wd_ref[...], preferred_element_type=F32)
    if final_norm:
        ms = jnp.mean(out * out, axis=-1, keepdims=True)
        out = out * lax.rsqrt(ms + EPS) * fin_ref[...]
    o_ref[...] = out


def _conv_ffn(x2, gain3, w_up, conv_w, conv_b3, w_down, fin_gain, layer, seq, tm, ts, final_norm):
    m, d = x2.shape
    hb = tm // SUBLANES
    n_halo = m // SUBLANES
    kern = functools.partial(_ffn_kernel, tiles_per_seq=seq // tm, final_norm=final_norm, ts=ts)
    resident = lambda a: pl.BlockSpec((None,) + a.shape[1:], lambda i: (layer, 0, 0),
                                      pipeline_mode=pl.Buffered(1))
    return pl.pallas_call(
        kern,
        grid=(m // tm,),
        in_specs=[
            pl.BlockSpec((tm, d), lambda i: (i, 0)),
            pl.BlockSpec((SUBLANES, d), lambda i: (jnp.maximum(i * hb - 1, 0), 0)),
            pl.BlockSpec((SUBLANES, d), lambda i: (jnp.minimum((i + 1) * hb, n_halo - 1), 0)),
            pl.BlockSpec((None, 1, d), lambda i: (layer, 0, 0)),
            resident(w_up), resident(conv_w), resident(conv_b3), resident(w_down),
            pl.BlockSpec((1, d), lambda i: (0, 0)),
        ],
        out_specs=pl.BlockSpec((tm, d), lambda i: (i, 0)),
        out_shape=jax.ShapeDtypeStruct((m, d), F32),
        scratch_shapes=[
            pltpu.VMEM((tm + 2 * SUBLANES, d), BF16),
            pltpu.VMEM((tm, D_FF), BF16),
        ],
        compiler_params=_cparams("parallel"),
        name="conv_ffn",
    )(x2, x2, x2, gain3, w_up, conv_w, conv_b3, w_down, fin_gain.reshape(1, d))


def kernel(x, mem, norm_mix, w_in, hg_lb_logits, hg_gnorm, w_hg_o, na_rpb, w_na_o, mem_norm,
           w_mem_kv, w_ca_o, w_out, norm_ffn, w_up, conv_w, conv_b, w_down, norm_final):
    batch, seq, d = x.shape
    mem_len = mem.shape[1]
    depth = w_in.shape[0]
    assert d == D_MODEL and seq % GRID_W == 0 and seq % HG_CHUNK == 0
    hgw = HG_HEADS * HG_DIM

    p_lb = jax.nn.softmax(hg_lb_logits.astype(F32), axis=0)
    lower_bounds = jnp.clip(jnp.cumsum(p_lb, axis=0) - p_lb[0], 0.0, 1.0)

    bf = lambda a: a.astype(BF16)
    w_in, w_mem_kv, w_hg_o, w_na_o, w_ca_o, w_out, w_up, w_down = map(
        bf, (w_in, w_mem_kv, w_hg_o, w_na_o, w_ca_o, w_out, w_up, w_down))
    vec3 = lambda a: a.reshape(depth, 1, -1)
    norm_mix3, norm_ffn3, gnorm3, conv_b3 = map(vec3, (norm_mix, norm_ffn, hg_gnorm, conv_b))
    lb3 = vec3(lower_bounds)
    mem_norm3 = jnp.broadcast_to(mem_norm.reshape(1, 1, d), (depth, 1, d))
    bias_tab = _natten_bias_table(na_rpb, seq // GRID_W)

    assert w_in.shape[-1] == (N_CB + 2 * hgw // LANES) * LANES

    x2 = x.reshape(batch * seq, d)
    mem2 = mem.reshape(batch * mem_len, d)

    for l in range(depth):
        z3, k3 = _in_proj(x2, norm_mix3, w_in, lb3, l, tm=1024, tn=2 * hgw)
        kv3 = _norm_matmul(mem2, mem_norm3, w_mem_kv, l, tm=batch * mem_len,
                           tn=w_mem_kv.shape[-1])

        ohg3 = _hgrn2(z3, k3, gnorm3, l, batch, seq)
        ona3 = _natten(z3, bias_tab, l, batch, seq)
        oca3 = _memattn(z3, kv3, batch, seq, mem_len, tq=512)

        x2 = _merge(x2, ohg3, ona3, oca3, z3, w_hg_o, w_na_o, w_ca_o, w_out, l, tm=512)
        x2 = _conv_ffn(x2, norm_ffn3, w_up, conv_w, conv_b3, w_down, norm_final, l, seq,
                       tm=512, ts=256, final_norm=(l == depth - 1))
    return x2.reshape(batch, seq, d)
```

```python
import functools

import numpy as np
import jax
import jax.numpy as jnp
from jax import lax
from jax.experimental import pallas as pl
from jax.experimental.pallas import tpu as pltpu

D_MODEL = 1024
GRID_W = 64
HG_HEADS = 8
HG_DIM = 128
NA_HEADS = 8
NA_DIM = 64
NA_KH = 8
NA_KW = 16
CA_HEADS = 4
CA_DIM = 128
D_FF = 2816
CONV_W = 3
EPS = 1e-6
F_FLOOR = 1e-12
MASK_NEG = -1e30

LANES = 128
SUBLANES = 8
VMEM_LIMIT = 56 * 1024 * 1024

HG_CHUNK = 128
HG_UNROLL = 8
NA_GROUP = 8

CB_HQ, CB_HI, CB_HG = 0, 8, 16
CB_NQ, CB_NK, CB_NV = 24, 28, 32
CB_CQ = 36
CB_GHG, CB_GNA, CB_GCA = 40, 48, 56
N_CB = 64

F32 = jnp.float32
BF16 = jnp.bfloat16


def _cparams(*sem):
    return pltpu.CompilerParams(dimension_semantics=sem, vmem_limit_bytes=VMEM_LIMIT)


def _sigmoid(x):
    return 1.0 / (1.0 + jnp.exp(-x))


PROJ_SUB = 512


def _normalise_rows(x_ref, g_ref, h_ref):
    @pl.when(pl.program_id(1) == 0)
    def _():
        x = x_ref[...]
        ms = jnp.mean(x * x, axis=-1, keepdims=True)
        h_ref[...] = (x * lax.rsqrt(ms + EPS) * g_ref[...]).astype(BF16)


def _sub_tiles(h_ref, w_ref):
    for s in range(w_ref.shape[1] // PROJ_SUB):
        acc = jnp.dot(h_ref[...], w_ref[:, s * PROJ_SUB:(s + 1) * PROJ_SUB],
                      preferred_element_type=F32)
        yield s * (PROJ_SUB // LANES), acc


def _norm_mm_kernel(x_ref, g_ref, w_ref, o_ref, h_ref):
    _normalise_rows(x_ref, g_ref, h_ref)
    for cb0, acc in _sub_tiles(h_ref, w_ref):
        for jj in range(PROJ_SUB // LANES):
            o_ref[cb0 + jj] = acc[:, jj * LANES:(jj + 1) * LANES].astype(o_ref.dtype)


def _norm_matmul(x2, gain3, w_bf16, layer, tm, tn):
    m, d = x2.shape
    n = w_bf16.shape[-1]
    return pl.pallas_call(
        _norm_mm_kernel,
        grid=(m // tm, n // tn),
        in_specs=[
            pl.BlockSpec((tm, d), lambda i, j: (i, 0)),
            pl.BlockSpec((None, 1, d), lambda i, j: (layer, 0, 0)),
            pl.BlockSpec((None, d, tn), lambda i, j: (layer, 0, j)),
        ],
        out_specs=pl.BlockSpec((tn // LANES, tm, LANES), lambda i, j: (j, i, 0)),
        out_shape=jax.ShapeDtypeStruct((n // LANES, m, LANES), BF16),
        scratch_shapes=[pltpu.VMEM((tm, d), BF16)],
        compiler_params=_cparams("parallel", "arbitrary"),
        name="norm_matmul",
    )(x2, gain3, w_bf16)


IN_TILE_Q, IN_TILE_FORGET, IN_TILE_GATE = 0, 1, 2


def _in_proj_kernel(x_ref, g_ref, w_ref, lb_ref, z_ref, k_ref, h_ref):
    _normalise_rows(x_ref, g_ref, h_ref)
    j = pl.program_id(1)
    half = z_ref.shape[0] // 2

    has_silu = jnp.logical_or(j == IN_TILE_Q, j == IN_TILE_GATE)

    @pl.when(has_silu)
    def _():
        scale = jnp.where(j == IN_TILE_Q, HG_DIM ** -0.5, 1.0)
        for cb0, acc in _sub_tiles(h_ref, w_ref):
            for jj in range(PROJ_SUB // LANES):
                z = acc[:, jj * LANES:(jj + 1) * LANES]
                if cb0 + jj < half:
                    z = z * _sigmoid(z) * scale
                z_ref[cb0 + jj] = z.astype(z_ref.dtype)

    @pl.when(j > IN_TILE_GATE)
    def _():
        for cb0, acc in _sub_tiles(h_ref, w_ref):
            for jj in range(PROJ_SUB // LANES):
                z_ref[cb0 + jj] = acc[:, jj * LANES:(jj + 1) * LANES].astype(z_ref.dtype)

    @pl.when(j == IN_TILE_FORGET)
    def _():
        for cb0, acc in _sub_tiles(h_ref, w_ref):
            for jj in range(PROJ_SUB // LANES):
                z = acc[:, jj * LANES:(jj + 1) * LANES]
                cb = cb0 + jj
                t = jnp.exp(-jnp.abs(z))
                r = 1.0 / (1.0 + t)
                sig_neg = jnp.where(z >= 0.0, t * r, r)
                k_ref[cb] = (1.0 - lb_ref[:, cb * LANES:(cb + 1) * LANES]) * sig_neg


def _in_proj(x2, gain3, w_bf16, lb3, layer, tm, tn):
    m, d = x2.shape
    n_tiles = w_bf16.shape[-1] // tn
    ncb = tn // LANES
    z_tile = lambda j: j - (j >= IN_TILE_FORGET).astype(jnp.int32)
    return pl.pallas_call(
        _in_proj_kernel,
        grid=(m // tm, n_tiles),
        in_specs=[
            pl.BlockSpec((tm, d), lambda i, j: (i, 0)),
            pl.BlockSpec((None, 1, d), lambda i, j: (layer, 0, 0)),
            pl.BlockSpec((None, d, tn), lambda i, j: (layer, 0, j)),
            pl.BlockSpec((None, 1, tn), lambda i, j: (layer, 0, 0)),
        ],
        out_specs=[pl.BlockSpec((ncb, tm, LANES), lambda i, j: (z_tile(j), i, 0)),
                   pl.BlockSpec((ncb, tm, LANES), lambda i, j: (0, i, 0))],
        out_shape=[jax.ShapeDtypeStruct(((n_tiles - 1) * ncb, m, LANES), BF16),
                   jax.ShapeDtypeStruct((ncb, m, LANES), F32)],
        scratch_shapes=[pltpu.VMEM((tm, d), BF16)],
        compiler_params=_cparams("parallel", "arbitrary"),
        name="in_proj",
    )(x2, gain3, w_bf16, lb3)


def _slabs(x):
    return [x[i:i + SUBLANES, :] for i in range(0, x.shape[0], SUBLANES)]


def _join(slabs):
    return jnp.concatenate(slabs, axis=0)


def _pair_scores(q_slabs, k_slabs):
    k_t = jnp.transpose(_join(k_slabs)).astype(BF16)
    return jnp.dot(_join(q_slabs).astype(BF16), k_t, preferred_element_type=F32)


def _double_blocks(qe, kx, tot, b, row8, reverse):
    n = len(qe)
    if b < SUBLANES:
        later = (row8 & b) != 0
        if reverse:
            later = jnp.logical_not(later)
        back, fwd = (SUBLANES - b, b) if reverse else (b, SUBLANES - b)
        prev = [pltpu.roll(t, back, 0) for t in tot]
        nxt = prev if 2 * b == SUBLANES else [pltpu.roll(t, fwd, 0) for t in tot]
        qe = [qe[i] * jnp.where(later, prev[i], 1.0) for i in range(n)]
        kx = [kx[i] * jnp.where(later, 1.0, nxt[i]) for i in range(n)]
        if 2 * b == SUBLANES:
            tot = [tot[i] * prev[i] for i in range(n)]
        else:
            tot = [tot[i] * jnp.where(later, prev[i], nxt[i]) for i in range(n)]
        return qe, kx, tot
    m = b // SUBLANES
    qe, kx, merged = list(qe), list(kx), []
    for p in range(0, n, 2 * m):
        first, second = (p + m, p) if reverse else (p, p + m)
        t_first, t_second = tot[first // m], tot[second // m]
        for i in range(m):
            qe[second + i] = qe[second + i] * t_first
            kx[first + i] = kx[first + i] * t_second
        merged.append(t_first * t_second)
    return qe, kx, merged


def _hgrn2_kernel(q_ref, v_ref, g_ref, kf_ref, kb_ref, gn_ref, o_ref,
                  oi_ref, qf_ref, qb_ref, kvf_ref, kvb_ref, df_ref, db_ref, sf_ref, sb_ref):
    c = HG_CHUNK
    nc = q_ref.shape[0] // c
    nt = (((1,), (1,)), ((), ()))
    tn = (((0,), (0,)), ((), ()))
    row8 = lax.broadcasted_iota(jnp.int32, (SUBLANES, LANES), 0)
    pair_xor = (lax.broadcasted_iota(jnp.int32, (c, c), 0)
                ^ lax.broadcasted_iota(jnp.int32, (c, c), 1))
    n_slab = c // SUBLANES

    def intra(ci, carry):
        rows = pl.ds(pl.multiple_of(ci * c, c), c)
        q = _slabs(q_ref[rows, :].astype(F32))
        v = v_ref[rows, :]
        k_f = _slabs(kf_ref[rows, :])
        k_b = _slabs(kb_ref[rows, :])
        f_f = [jnp.maximum(1.0 - k, F_FLOOR) for k in k_f]
        f_b = [jnp.maximum(1.0 - k, F_FLOOR) for k in k_b]
        fwd = ([q[i] * f_f[i] for i in range(n_slab)], k_f, f_f)
        bwd = ([q[i] * f_b[i] for i in range(n_slab)], k_b, f_b)

        k_both = [k_f[i] + k_b[i] for i in range(n_slab)]
        scores = _pair_scores(q, k_both)
        b = 1
        while b < c:
            (q_f, x_f, _), (q_b, x_b, _) = fwd, bwd
            if b >= SUBLANES:
                is_right = [(i // (b // SUBLANES)) % 2 == 1 for i in range(n_slab)]
                qh = [q_f[i] if is_right[i] else q_b[i] for i in range(n_slab)]
                kh = [x_b[i] if is_right[i] else x_f[i] for i in range(n_slab)]
            else:
                right = (row8 & b) != 0
                qh = [jnp.where(right, q_f[i], q_b[i]) for i in range(n_slab)]
                kh = [jnp.where(right, x_b[i], x_f[i]) for i in range(n_slab)]
            r = _pair_scores(qh, kh)
            scores = jnp.where(pair_xor >= b, r, scores)
            fwd = _double_blocks(*fwd, b, row8, False)
            bwd = _double_blocks(*bwd, b, row8, True)
            b *= 2
        oi_ref[rows, :] = jnp.dot(scores.astype(BF16), v, preferred_element_type=F32)

        (q_f, x_f, tot_f), (q_b, x_b, tot_b) = fwd, bwd
        qf_ref[rows, :] = _join(q_f).astype(BF16)
        qb_ref[rows, :] = _join(q_b).astype(BF16)
        kvf_ref[ci] = lax.dot_general(v, _join(x_f).astype(BF16), tn, preferred_element_type=F32)
        kvb_ref[ci] = lax.dot_general(v, _join(x_b).astype(BF16), tn, preferred_element_type=F32)
        df_ref[ci] = tot_f[0]
        db_ref[ci] = tot_b[0]
        return carry

    lax.fori_loop(0, nc, intra, 0, unroll=HG_UNROLL)

    def scan_states(i, carry):
        s_f, s_b = carry
        cb = nc - 1 - i
        sf_ref[i] = s_f.astype(BF16)
        sb_ref[cb] = s_b.astype(BF16)
        s_f = s_f * df_ref[i][0:1, :] + kvf_ref[i]
        s_b = s_b * db_ref[cb][0:1, :] + kvb_ref[cb]
        return s_f, s_b

    zero = jnp.zeros((HG_DIM, HG_DIM), F32)
    lax.fori_loop(0, nc, scan_states, (zero, zero))

    gn = gn_ref[...]

    def finish(ci, carry):
        rows = pl.ds(pl.multiple_of(ci * c, c), c)
        o = oi_ref[rows, :]
        o = o + lax.dot_general(qf_ref[rows, :], sf_ref[ci], nt, preferred_element_type=F32)
        o = o + lax.dot_general(qb_ref[rows, :], sb_ref[ci], nt, preferred_element_type=F32)
        ms = jnp.mean(o * o, axis=-1, keepdims=True)
        o = o * lax.rsqrt(ms + EPS) * gn
        o_ref[rows, :] = (o * g_ref[rows, :].astype(F32)).astype(o_ref.dtype)
        return carry

    lax.fori_loop(0, nc, finish, 0, unroll=HG_UNROLL)


def _hgrn2(z3, k3, gnorm3, layer, batch, seq):
    c = HG_CHUNK
    nc = seq // c
    blk = lambda cb0: pl.BlockSpec((None, seq, LANES), lambda b, h: (cb0 + h, b, 0))
    return pl.pallas_call(
        _hgrn2_kernel,
        grid=(batch, HG_HEADS),
        in_specs=[
            blk(CB_HQ), blk(CB_HI), blk(CB_HG),
            blk(0), blk(HG_HEADS),
            pl.BlockSpec((None, 1, LANES), lambda b, h: (layer, 0, 0)),
        ],
        out_specs=pl.BlockSpec((None, seq, LANES), lambda b, h: (h, b, 0)),
        out_shape=jax.ShapeDtypeStruct((HG_HEADS, batch * seq, LANES), BF16),
        scratch_shapes=[
            pltpu.VMEM((seq, LANES), F32),
            pltpu.VMEM((seq, LANES), BF16),
            pltpu.VMEM((seq, LANES), BF16),
            pltpu.VMEM((nc, HG_DIM, HG_DIM), F32),
            pltpu.VMEM((nc, HG_DIM, HG_DIM), F32),
            pltpu.VMEM((nc, SUBLANES, LANES), F32),
            pltpu.VMEM((nc, SUBLANES, LANES), F32),
            pltpu.VMEM((nc, HG_DIM, HG_DIM), BF16),
            pltpu.VMEM((nc, HG_DIM, HG_DIM), BF16),
        ],
        compiler_params=_cparams("parallel", "parallel"),
        name="hgrn2",
    )(z3, z3, z3, k3, k3, gnorm3)


def _natten_kernel(q_ref, k_ref, v_ref, bias_ref, o_ref):
    rows = q_ref.shape[0] // GRID_W
    kh = min(NA_KH, rows)
    win = kh * GRID_W
    nt = (((1,), (1,)), ((), ()))
    lane = lax.broadcasted_iota(jnp.int32, (GRID_W, LANES), 1)
    first = lane < NA_DIM

    def group(gi, carry):
        scored = []
        for u in range(NA_GROUP):
            r = gi * NA_GROUP + u
            rs = jnp.clip(r - kh // 2, 0, rows - kh)
            q2 = q_ref[pl.ds(pl.multiple_of(r * GRID_W, GRID_W), GRID_W), :] * (NA_DIM ** -0.5)
            zero = jnp.zeros_like(q2)
            q_st = jnp.concatenate([jnp.where(first, q2, zero), jnp.where(first, zero, q2)], axis=0)
            kw = k_ref[pl.ds(pl.multiple_of(rs * GRID_W, GRID_W), win), :]
            scored.append((lax.dot_general(q_st, kw, nt, preferred_element_type=F32), r, rs))
        probs = []
        for s, r, rs in scored:
            bm = jnp.concatenate([bias_ref[0, r - rs], bias_ref[1, r - rs]], axis=0)
            s = jnp.where(bm > 0.5 * MASK_NEG, s + bm, MASK_NEG)
            m = jnp.max(s, axis=-1, keepdims=True)
            p = jnp.exp(s - m)
            probs.append((p.astype(BF16), jnp.sum(p, axis=-1, keepdims=True), r, rs))
        for p, l, r, rs in probs:
            vw = v_ref[pl.ds(pl.multiple_of(rs * GRID_W, GRID_W), win), :]
            o_st = jnp.dot(p, vw, preferred_element_type=F32) / l
            o = jnp.where(first, o_st[:GRID_W], o_st[GRID_W:])
            o_ref[pl.ds(pl.multiple_of(r * GRID_W, GRID_W), GRID_W), :] = o.astype(o_ref.dtype)
        return carry

    lax.fori_loop(0, rows // NA_GROUP, group, 0)


def _natten_bias_table(rpb, rows):
    kh = min(NA_KH, rows)
    r = np.arange(rows)
    rs = np.clip(r - kh // 2, 0, rows - kh)
    n_case = int((r - rs).max()) + 1
    case = np.arange(n_case)
    dr = np.arange(kh)[None, :] - case[:, None] + (NA_KH - 1)
    c = np.arange(GRID_W)
    col_start = np.clip(c - NA_KW // 2, 0, GRID_W - NA_KW)
    col_mask = (c[None, :] >= col_start[:, None]) & (c[None, :] < col_start[:, None] + NA_KW)
    dc = np.clip(c[None, :] - c[:, None], -(NA_KW - 1), NA_KW - 1) + (NA_KW - 1)
    depth, heads, _, n_dc = rpb.shape
    by_row = jnp.take(rpb.astype(F32).reshape(depth * heads, -1, n_dc), dr.reshape(-1), axis=1)
    by_row = by_row.reshape(depth * heads, n_case, kh, n_dc)
    pick = ((dc[None, :, :] == np.arange(n_dc)[:, None, None]) & col_mask[None]).astype(np.float32)
    masked = np.where(col_mask, 0.0, MASK_NEG).astype(np.float32)
    tab = jnp.einsum("xcjd,dqk->xcqjk", by_row, pick, precision=lax.Precision.HIGHEST)
    tab = tab + masked[None, None, :, None, :]
    return tab.reshape(depth, heads, n_case, GRID_W, kh * GRID_W)


def _natten(z3, bias_tab, layer, batch, seq):
    n_case, win = bias_tab.shape[2], bias_tab.shape[4]
    blk = lambda cb0: pl.BlockSpec((None, seq, LANES), lambda b, p: (cb0 + p, b, 0))
    return pl.pallas_call(
        _natten_kernel,
        grid=(batch, NA_HEADS // 2),
        in_specs=[
            blk(CB_NQ), blk(CB_NK), blk(CB_NV),
            pl.BlockSpec((None, 2, n_case, GRID_W, win), lambda b, p: (layer, p, 0, 0, 0)),
        ],
        out_specs=pl.BlockSpec((None, seq, LANES), lambda b, p: (p, b, 0)),
        out_shape=jax.ShapeDtypeStruct((NA_HEADS // 2, batch * seq, LANES), BF16),
        compiler_params=_cparams("parallel", "parallel"),
        name="natten",
    )(z3, z3, z3, bias_tab)


def _memattn_kernel(q_ref, kv_ref, o_ref):
    nt = (((1,), (1,)), ((), ()))
    for h in range(CA_HEADS):
        s = lax.dot_general(q_ref[h], kv_ref[h], nt, preferred_element_type=F32) * (CA_DIM ** -0.5)
        m = jnp.max(s, axis=-1, keepdims=True)
        p = jnp.exp(s - m)
        l = jnp.sum(p, axis=-1, keepdims=True)
        o = jnp.dot(p.astype(BF16), kv_ref[CA_HEADS + h], preferred_element_type=F32) / l
        o_ref[h] = o.astype(o_ref.dtype)


def _memattn(z3, kv3, batch, seq, mem_len, tq):
    nq = seq // tq
    return pl.pallas_call(
        _memattn_kernel,
        grid=(batch, nq),
        in_specs=[
            pl.BlockSpec((CA_HEADS, tq, LANES), lambda b, i: (CB_CQ // CA_HEADS, b * nq + i, 0)),
            pl.BlockSpec((2 * CA_HEADS, mem_len, LANES), lambda b, i: (0, b, 0)),
        ],
        out_specs=pl.BlockSpec((CA_HEADS, tq, LANES), lambda b, i: (0, b * nq + i, 0)),
        out_shape=jax.ShapeDtypeStruct((CA_HEADS, batch * seq, LANES), BF16),
        compiler_params=_cparams("parallel", "parallel"),
        name="memattn",
    )(z3, kv3)


def _cat(ref):
    return jnp.concatenate([ref[j] for j in range(ref.shape[0])], axis=-1)


def _merge_kernel(x_ref, ohg_ref, ona_ref, oca_ref, ghg_ref, gna_ref, gca_ref,
                  whg_ref, wna_ref, wca_ref, wout_ref, o_ref):
    def branch(o3_ref, w_ref, g_ref):
        y = jnp.dot(_cat(o3_ref), w_ref[...], preferred_element_type=F32)
        return _sigmoid(_cat(g_ref).astype(F32)) * y

    merged = branch(ohg_ref, whg_ref, ghg_ref)
    merged = merged + branch(ona_ref, wna_ref, gna_ref)
    merged = merged + branch(oca_ref, wca_ref, gca_ref)
    o_ref[...] = x_ref[...] + jnp.dot(merged.astype(BF16), wout_ref[...],
                                      preferred_element_type=F32)


def _merge(x2, ohg3, ona3, oca3, z3, w_hg_o, w_na_o, w_ca_o, w_out, layer, tm):
    m, d = x2.shape
    ncb = d // LANES
    act = lambda n: pl.BlockSpec((n, tm, LANES), lambda i: (0, i, 0))
    gate = lambda cb0: pl.BlockSpec((ncb, tm, LANES), lambda i: (cb0 // ncb, i, 0))
    full = lambda a: pl.BlockSpec((None,) + a.shape[1:], lambda i: (layer, 0, 0),
                                  pipeline_mode=pl.Buffered(1))
    return pl.pallas_call(
        _merge_kernel,
        grid=(m // tm,),
        in_specs=[
            pl.BlockSpec((tm, d), lambda i: (i, 0)),
            act(ohg3.shape[0]), act(ona3.shape[0]), act(oca3.shape[0]),
            gate(CB_GHG), gate(CB_GNA), gate(CB_GCA),
            full(w_hg_o), full(w_na_o), full(w_ca_o), full(w_out),
        ],
        out_specs=pl.BlockSpec((tm, d), lambda i: (i, 0)),
        out_shape=jax.ShapeDtypeStruct((m, d), F32),
        compiler_params=_cparams("parallel"),
        name="merge",
    )(x2, ohg3, ona3, oca3, z3, z3, z3, w_hg_o, w_na_o, w_ca_o, w_out)


def _ffn_kernel(x_ref, xp_ref, xn_ref, gain_ref, wup_ref, cw_ref, cb_ref, wd_ref, fin_ref, o_ref,
                h_ref, y_ref, *, tiles_per_seq, final_norm, ts):
    i = pl.program_id(0)
    tm = x_ref.shape[0]
    halo = SUBLANES

    def normed(x):
        ms = jnp.mean(x * x, axis=-1, keepdims=True)
        return x * lax.rsqrt(ms + EPS) * gain_ref[...]

    keep_prev = (i % tiles_per_seq != 0).astype(F32)
    keep_next = (i % tiles_per_seq != tiles_per_seq - 1).astype(F32)
    h_ref[0:halo, :] = (normed(xp_ref[...]) * keep_prev).astype(BF16)
    h_ref[halo:halo + tm, :] = normed(x_ref[...]).astype(BF16)
    h_ref[halo + tm:, :] = (normed(xn_ref[...]) * keep_next).astype(BF16)

    def conv_proj(cols):
        u = jnp.dot(h_ref[...], wup_ref[:, cols], preferred_element_type=F32)
        rows = u.shape[0]
        prev = pltpu.roll(u, 1, 0)[halo:halo + tm]
        nxt = pltpu.roll(u, rows - 1, 0)[halo:halo + tm]
        out = prev * cw_ref[0:1, cols]
        out = out + u[halo:halo + tm] * cw_ref[1:2, cols]
        out = out + nxt * cw_ref[2:3, cols]
        return out + cb_ref[:, cols]

    def sub_tile(s, carry):
        cols_a = pl.ds(pl.multiple_of(s * ts, ts), ts)
        cols_g = pl.ds(pl.multiple_of(D_FF + s * ts, ts), ts)
        a = conv_proj(cols_a)
        g = conv_proj(cols_g)
        c = np.float32(np.sqrt(2 / np.pi))
        th = jnp.tanh(a * ((a * a) * (c * np.float32(0.044715)) + c))
        half_ag = (a * g) * 0.5
        y_ref[:, cols_a] = (half_ag + half_ag * th).astype(BF16)
        return carry

    lax.fori_loop(0, D_FF // ts, sub_tile, 0, unroll=True)

    out = x_ref[...] + jnp.dot(y_ref[...], wd_ref[...], preferred_element_type=F32)
    if final_norm:
        ms = jnp.mean(out * out, axis=-1, keepdims=True)
        out = out * lax.rsqrt(ms + EPS) * fin_ref[...]
    o_ref[...] = out


def _conv_ffn(x2, gain3, w_up, conv_w, conv_b3, w_down, fin_gain, layer, seq, tm, ts, final_norm):
    m, d = x2.shape
    hb = tm // SUBLANES
    n_halo = m // SUBLANES
    kern = functools.partial(_ffn_kernel, tiles_per_seq=seq // tm, final_norm=final_norm, ts=ts)
    resident = lambda a: pl.BlockSpec((None,) + a.shape[1:], lambda i: (layer, 0, 0),
                                      pipeline_mode=pl.Buffered(1))
    return pl.pallas_call(
        kern,
        grid=(m // tm,),
        in_specs=[
            pl.BlockSpec((tm, d), lambda i: (i, 0)),
            pl.BlockSpec((SUBLANES, d), lambda i: (jnp.maximum(i * hb - 1, 0), 0)),
            pl.BlockSpec((SUBLANES, d), lambda i: (jnp.minimum((i + 1) * hb, n_halo - 1), 0)),
            pl.BlockSpec((None, 1, d), lambda i: (layer, 0, 0)),
            resident(w_up), resident(conv_w), resident(conv_b3), resident(w_down),
            pl.BlockSpec((1, d), lambda i: (0, 0)),
        ],
        out_specs=pl.BlockSpec((tm, d), lambda i: (i, 0)),
        out_shape=jax.ShapeDtypeStruct((m, d), F32),
        scratch_shapes=[
            pltpu.VMEM((tm + 2 * SUBLANES, d), BF16),
            pltpu.VMEM((tm, D_FF), BF16),
        ],
        compiler_params=_cparams("parallel"),
        name="conv_ffn",
    )(x2, x2, x2, gain3, w_up, conv_w, conv_b3, w_down, fin_gain.reshape(1, d))


def kernel(x, mem, norm_mix, w_in, hg_lb_logits, hg_gnorm, w_hg_o, na_rpb, w_na_o, mem_norm,
           w_mem_kv, w_ca_o, w_out, norm_ffn, w_up, conv_w, conv_b, w_down, norm_final):
    batch, seq, d = x.shape
    mem_len = mem.shape[1]
    depth = w_in.shape[0]
    assert d == D_MODEL and seq % GRID_W == 0 and seq % HG_CHUNK == 0
    hgw = HG_HEADS * HG_DIM

    p_lb = jax.nn.softmax(hg_lb_logits.astype(F32), axis=0)
    lower_bounds = jnp.clip(jnp.cumsum(p_lb, axis=0) - p_lb[0], 0.0, 1.0)

    bf = lambda a: a.astype(BF16)
    w_in, w_mem_kv, w_hg_o, w_na_o, w_ca_o, w_out, w_up, w_down = map(
        bf, (w_in, w_mem_kv, w_hg_o, w_na_o, w_ca_o, w_out, w_up, w_down))
    vec3 = lambda a: a.reshape(depth, 1, -1)
    norm_mix3, norm_ffn3, gnorm3, conv_b3 = map(vec3, (norm_mix, norm_ffn, hg_gnorm, conv_b))
    lb3 = vec3(lower_bounds)
    mem_norm3 = jnp.broadcast_to(mem_norm.reshape(1, 1, d), (depth, 1, d))
    bias_tab = _natten_bias_table(na_rpb, seq // GRID_W)

    assert w_in.shape[-1] == (N_CB + 2 * hgw // LANES) * LANES

    x2 = x.reshape(batch * seq, d)
    mem2 = mem.reshape(batch * mem_len, d)

    for l in range(depth):
        z3, k3 = _in_proj(x2, norm_mix3, w_in, lb3, l, tm=1024, tn=2 * hgw)
        kv3 = _norm_matmul(mem2, mem_norm3, w_mem_kv, l, tm=batch * mem_len,
                           tn=w_mem_kv.shape[-1])

        ohg3 = _hgrn2(z3, k3, gnorm3, l, batch, seq)
        ona3 = _natten(z3, bias_tab, l, batch, seq)
        oca3 = _memattn(z3, kv3, batch, seq, mem_len, tq=512)

        x2 = _merge(x2, ohg3, ona3, oca3, z3, w_hg_o, w_na_o, w_ca_o, w_out, l, tm=1024)
        x2 = _conv_ffn(x2, norm_ffn3, w_up, conv_w, conv_b3, w_down, norm_final, l, seq,
                       tm=1024, ts=256, final_norm=(l == depth - 1))
    return x2.reshape(batch, seq, d)
```

```python
import functools

import numpy as np
import jax
import jax.numpy as jnp
from jax import lax
from jax.experimental import pallas as pl
from jax.experimental.pallas import tpu as pltpu

D_MODEL = 1024
GRID_W = 64
HG_HEADS = 8
HG_DIM = 128
NA_HEADS = 8
NA_DIM = 64
NA_KH = 8
NA_KW = 16
CA_HEADS = 4
CA_DIM = 128
D_FF = 2816
CONV_W = 3
EPS = 1e-6
F_FLOOR = 1e-12
MASK_NEG = -1e30

LANES = 128
SUBLANES = 8
VMEM_LIMIT = 56 * 1024 * 1024

ROW_TILE = 1024
FFN_SUB = 256
MEMATTN_ROWS = 512
HG_CHUNK = 128
HG_UNROLL = 8
NA_GROUP = 8

CB_HQ, CB_HI, CB_HG = 0, 8, 16
CB_NQ, CB_NK, CB_NV = 24, 28, 32
CB_CQ = 36
CB_GHG, CB_GNA, CB_GCA = 40, 48, 56
N_CB = 64

F32 = jnp.float32
BF16 = jnp.bfloat16


def _cparams(*sem):
    return pltpu.CompilerParams(dimension_semantics=sem, vmem_limit_bytes=VMEM_LIMIT)


def _sigmoid(x):
    return 1.0 / (1.0 + jnp.exp(-x))


PROJ_SUB = 512


def _normalise_rows(x_ref, g_ref, h_ref, weight_axis=1):
    @pl.when(pl.program_id(weight_axis) == 0)
    def _():
        x = x_ref[...]
        ms = jnp.mean(x * x, axis=-1, keepdims=True)
        h_ref[...] = (x * lax.rsqrt(ms + EPS) * g_ref[...]).astype(BF16)


def _sub_tiles(h_ref, w_ref):
    for s in range(w_ref.shape[1] // PROJ_SUB):
        acc = jnp.dot(h_ref[...], w_ref[:, s * PROJ_SUB:(s + 1) * PROJ_SUB],
                      preferred_element_type=F32)
        yield s * (PROJ_SUB // LANES), acc


def _mem_kv_kernel(x_ref, g_ref, w_ref, o_ref, h_ref):
    _normalise_rows(x_ref, g_ref, h_ref, weight_axis=0)
    for cb0, acc in _sub_tiles(h_ref, w_ref):
        for jj in range(PROJ_SUB // LANES):
            o_ref[cb0 + jj] = acc[:, jj * LANES:(jj + 1) * LANES].astype(o_ref.dtype)


def _mem_kv_proj(mem2, gain, w_bf16):
    m, d = mem2.shape
    depth, _, n = w_bf16.shape
    return pl.pallas_call(
        _mem_kv_kernel,
        grid=(depth,),
        in_specs=[
            pl.BlockSpec((m, d), lambda l: (0, 0)),
            pl.BlockSpec((1, d), lambda l: (0, 0)),
            pl.BlockSpec((None, d, n), lambda l: (l, 0, 0)),
        ],
        out_specs=pl.BlockSpec((None, n // LANES, m, LANES), lambda l: (l, 0, 0, 0)),
        out_shape=jax.ShapeDtypeStruct((depth, n // LANES, m, LANES), BF16),
        scratch_shapes=[pltpu.VMEM((m, d), BF16)],
        compiler_params=_cparams("arbitrary"),
        name="mem_kv_proj",
    )(mem2, gain.reshape(1, d), w_bf16)


IN_TILE_Q, IN_TILE_FORGET, IN_TILE_GATE = 0, 1, 2


def _in_proj_kernel(x_ref, g_ref, w_ref, lb_ref, z_ref, k_ref, h_ref):
    _normalise_rows(x_ref, g_ref, h_ref)
    j = pl.program_id(1)
    half = z_ref.shape[0] // 2

    has_silu = jnp.logical_or(j == IN_TILE_Q, j == IN_TILE_GATE)

    @pl.when(has_silu)
    def _():
        scale = jnp.where(j == IN_TILE_Q, HG_DIM ** -0.5, 1.0)
        for cb0, acc in _sub_tiles(h_ref, w_ref):
            for jj in range(PROJ_SUB // LANES):
                z = acc[:, jj * LANES:(jj + 1) * LANES]
                if cb0 + jj < half:
                    z = z * _sigmoid(z) * scale
                z_ref[cb0 + jj] = z.astype(z_ref.dtype)

    @pl.when(j > IN_TILE_GATE)
    def _():
        for cb0, acc in _sub_tiles(h_ref, w_ref):
            for jj in range(PROJ_SUB // LANES):
                z_ref[cb0 + jj] = acc[:, jj * LANES:(jj + 1) * LANES].astype(z_ref.dtype)

    @pl.when(j == IN_TILE_FORGET)
    def _():
        for cb0, acc in _sub_tiles(h_ref, w_ref):
            for jj in range(PROJ_SUB // LANES):
                z = acc[:, jj * LANES:(jj + 1) * LANES]
                cb = cb0 + jj
                t = jnp.exp(-jnp.abs(z))
                r = 1.0 / (1.0 + t)
                sig_neg = jnp.where(z >= 0.0, t * r, r)
                k_ref[cb] = (1.0 - lb_ref[:, cb * LANES:(cb + 1) * LANES]) * sig_neg


def _in_proj(x2, gain3, w_bf16, lb3, layer, tm, tn):
    m, d = x2.shape
    n_tiles = w_bf16.shape[-1] // tn
    ncb = tn // LANES
    z_tile = lambda j: j - (j >= IN_TILE_FORGET).astype(jnp.int32)
    return pl.pallas_call(
        _in_proj_kernel,
        grid=(m // tm, n_tiles),
        in_specs=[
            pl.BlockSpec((tm, d), lambda i, j: (i, 0)),
            pl.BlockSpec((None, 1, d), lambda i, j: (layer, 0, 0)),
            pl.BlockSpec((None, d, tn), lambda i, j: (layer, 0, j)),
            pl.BlockSpec((None, 1, tn), lambda i, j: (layer, 0, 0)),
        ],
        out_specs=[pl.BlockSpec((ncb, tm, LANES), lambda i, j: (z_tile(j), i, 0)),
                   pl.BlockSpec((ncb, tm, LANES), lambda i, j: (0, i, 0))],
        out_shape=[jax.ShapeDtypeStruct(((n_tiles - 1) * ncb, m, LANES), BF16),
                   jax.ShapeDtypeStruct((ncb, m, LANES), F32)],
        scratch_shapes=[pltpu.VMEM((tm, d), BF16)],
        compiler_params=_cparams("parallel", "arbitrary"),
        name="in_proj",
    )(x2, gain3, w_bf16, lb3)


def _slabs(x):
    return [x[i:i + SUBLANES, :] for i in range(0, x.shape[0], SUBLANES)]


def _join(slabs):
    return jnp.concatenate(slabs, axis=0)


def _pair_scores(q_slabs, k_slabs):
    k_t = jnp.transpose(_join(k_slabs)).astype(BF16)
    return jnp.dot(_join(q_slabs).astype(BF16), k_t, preferred_element_type=F32)


def _double_blocks(qe, kx, tot, b, row8, reverse):
    n = len(qe)
    if b < SUBLANES:
        later = (row8 & b) != 0
        if reverse:
            later = jnp.logical_not(later)
        back, fwd = (SUBLANES - b, b) if reverse else (b, SUBLANES - b)
        prev = [pltpu.roll(t, back, 0) for t in tot]
        nxt = prev if 2 * b == SUBLANES else [pltpu.roll(t, fwd, 0) for t in tot]
        qe = [qe[i] * jnp.where(later, prev[i], 1.0) for i in range(n)]
        kx = [kx[i] * jnp.where(later, 1.0, nxt[i]) for i in range(n)]
        if 2 * b == SUBLANES:
            tot = [tot[i] * prev[i] for i in range(n)]
        else:
            tot = [tot[i] * jnp.where(later, prev[i], nxt[i]) for i in range(n)]
        return qe, kx, tot
    m = b // SUBLANES
    qe, kx, merged = list(qe), list(kx), []
    for p in range(0, n, 2 * m):
        first, second = (p + m, p) if reverse else (p, p + m)
        t_first, t_second = tot[first // m], tot[second // m]
        for i in range(m):
            qe[second + i] = qe[second + i] * t_first
            kx[first + i] = kx[first + i] * t_second
        merged.append(t_first * t_second)
    return qe, kx, merged


def _hgrn2_kernel(q_ref, v_ref, g_ref, kf_ref, kb_ref, gn_ref, o_ref,
                  oi_ref, qf_ref, qb_ref, kvf_ref, kvb_ref, df_ref, db_ref, sf_ref, sb_ref):
    c = HG_CHUNK
    nc = q_ref.shape[0] // c
    nt = (((1,), (1,)), ((), ()))
    tn = (((0,), (0,)), ((), ()))
    row8 = lax.broadcasted_iota(jnp.int32, (SUBLANES, LANES), 0)
    pair_xor = (lax.broadcasted_iota(jnp.int32, (c, c), 0)
                ^ lax.broadcasted_iota(jnp.int32, (c, c), 1))
    n_slab = c // SUBLANES

    def intra(ci, carry):
        rows = pl.ds(pl.multiple_of(ci * c, c), c)
        q = _slabs(q_ref[rows, :].astype(F32))
        v = v_ref[rows, :]
        k_f = _slabs(kf_ref[rows, :])
        k_b = _slabs(kb_ref[rows, :])
        f_f = [jnp.maximum(1.0 - k, F_FLOOR) for k in k_f]
        f_b = [jnp.maximum(1.0 - k, F_FLOOR) for k in k_b]
        fwd = ([q[i] * f_f[i] for i in range(n_slab)], k_f, f_f)
        bwd = ([q[i] * f_b[i] for i in range(n_slab)], k_b, f_b)

        k_both = [k_f[i] + k_b[i] for i in range(n_slab)]
        scores = _pair_scores(q, k_both)
        b = 1
        while b < c:
            (q_f, x_f, _), (q_b, x_b, _) = fwd, bwd
            if b >= SUBLANES:
                is_right = [(i // (b // SUBLANES)) % 2 == 1 for i in range(n_slab)]
                qh = [q_f[i] if is_right[i] else q_b[i] for i in range(n_slab)]
                kh = [x_b[i] if is_right[i] else x_f[i] for i in range(n_slab)]
            else:
                right = (row8 & b) != 0
                qh = [jnp.where(right, q_f[i], q_b[i]) for i in range(n_slab)]
                kh = [jnp.where(right, x_b[i], x_f[i]) for i in range(n_slab)]
            r = _pair_scores(qh, kh)
            scores = jnp.where(pair_xor >= b, r, scores)
            fwd = _double_blocks(*fwd, b, row8, False)
            bwd = _double_blocks(*bwd, b, row8, True)
            b *= 2
        oi_ref[rows, :] = jnp.dot(scores.astype(BF16), v, preferred_element_type=F32)

        (q_f, x_f, tot_f), (q_b, x_b, tot_b) = fwd, bwd
        qf_ref[rows, :] = _join(q_f).astype(BF16)
        qb_ref[rows, :] = _join(q_b).astype(BF16)
        kvf_ref[ci] = lax.dot_general(v, _join(x_f).astype(BF16), tn, preferred_element_type=F32)
        kvb_ref[ci] = lax.dot_general(v, _join(x_b).astype(BF16), tn, preferred_element_type=F32)
        df_ref[ci] = tot_f[0]
        db_ref[ci] = tot_b[0]
        return carry

    lax.fori_loop(0, nc, intra, 0, unroll=HG_UNROLL)

    def scan_states(i, carry):
        s_f, s_b = carry
        cb = nc - 1 - i
        sf_ref[i] = s_f.astype(BF16)
        sb_ref[cb] = s_b.astype(BF16)
        s_f = s_f * df_ref[i][0:1, :] + kvf_ref[i]
        s_b = s_b * db_ref[cb][0:1, :] + kvb_ref[cb]
        return s_f, s_b

    zero = jnp.zeros((HG_DIM, HG_DIM), F32)
    lax.fori_loop(0, nc, scan_states, (zero, zero))

    gn = gn_ref[...]

    def finish(ci, carry):
        rows = pl.ds(pl.multiple_of(ci * c, c), c)
        o = oi_ref[rows, :]
        o = o + lax.dot_general(qf_ref[rows, :], sf_ref[ci], nt, preferred_element_type=F32)
        o = o + lax.dot_general(qb_ref[rows, :], sb_ref[ci], nt, preferred_element_type=F32)
        ms = jnp.mean(o * o, axis=-1, keepdims=True)
        o = o * lax.rsqrt(ms + EPS) * gn
        o_ref[rows, :] = (o * g_ref[rows, :].astype(F32)).astype(o_ref.dtype)
        return carry

    lax.fori_loop(0, nc, finish, 0, unroll=HG_UNROLL)


def _hgrn2(z3, k3, gnorm3, layer, batch, seq):
    c = HG_CHUNK
    nc = seq // c
    blk = lambda cb0: pl.BlockSpec((None, seq, LANES), lambda b, h: (cb0 + h, b, 0))
    return pl.pallas_call(
        _hgrn2_kernel,
        grid=(batch, HG_HEADS),
        in_specs=[
            blk(CB_HQ), blk(CB_HI), blk(CB_HG),
            blk(0), blk(HG_HEADS),
            pl.BlockSpec((None, 1, LANES), lambda b, h: (layer, 0, 0)),
        ],
        out_specs=pl.BlockSpec((None, seq, LANES), lambda b, h: (h, b, 0)),
        out_shape=jax.ShapeDtypeStruct((HG_HEADS, batch * seq, LANES), BF16),
        scratch_shapes=[
            pltpu.VMEM((seq, LANES), F32),
            pltpu.VMEM((seq, LANES), BF16),
            pltpu.VMEM((seq, LANES), BF16),
            pltpu.VMEM((nc, HG_DIM, HG_DIM), F32),
            pltpu.VMEM((nc, HG_DIM, HG_DIM), F32),
            pltpu.VMEM((nc, SUBLANES, LANES), F32),
            pltpu.VMEM((nc, SUBLANES, LANES), F32),
            pltpu.VMEM((nc, HG_DIM, HG_DIM), BF16),
            pltpu.VMEM((nc, HG_DIM, HG_DIM), BF16),
        ],
        compiler_params=_cparams("parallel", "parallel"),
        name="hgrn2",
    )(z3, z3, z3, k3, k3, gnorm3)


def _natten_kernel(q_ref, k_ref, v_ref, bias_ref, o_ref):
    rows = q_ref.shape[0] // GRID_W
    kh = min(NA_KH, rows)
    win = kh * GRID_W
    nt = (((1,), (1,)), ((), ()))
    lane = lax.broadcasted_iota(jnp.int32, (GRID_W, LANES), 1)
    first = lane < NA_DIM

    def group(gi, carry):
        scored = []
        for u in range(NA_GROUP):
            r = gi * NA_GROUP + u
            rs = jnp.clip(r - kh // 2, 0, rows - kh)
            q2 = q_ref[pl.ds(pl.multiple_of(r * GRID_W, GRID_W), GRID_W), :] * (NA_DIM ** -0.5)
            zero = jnp.zeros_like(q2)
            q_st = jnp.concatenate([jnp.where(first, q2, zero), jnp.where(first, zero, q2)], axis=0)
            kw = k_ref[pl.ds(pl.multiple_of(rs * GRID_W, GRID_W), win), :]
            scored.append((lax.dot_general(q_st, kw, nt, preferred_element_type=F32), r, rs))
        probs = []
        for s, r, rs in scored:
            bm = jnp.concatenate([bias_ref[0, r - rs], bias_ref[1, r - rs]], axis=0)
            s = jnp.where(bm > 0.5 * MASK_NEG, s + bm, MASK_NEG)
            m = jnp.max(s, axis=-1, keepdims=True)
            p = jnp.exp(s - m)
            probs.append((p.astype(BF16), jnp.sum(p, axis=-1, keepdims=True), r, rs))
        for p, l, r, rs in probs:
            vw = v_ref[pl.ds(pl.multiple_of(rs * GRID_W, GRID_W), win), :]
            o_st = jnp.dot(p, vw, preferred_element_type=F32) / l
            o = jnp.where(first, o_st[:GRID_W], o_st[GRID_W:])
            o_ref[pl.ds(pl.multiple_of(r * GRID_W, GRID_W), GRID_W), :] = o.astype(o_ref.dtype)
        return carry

    lax.fori_loop(0, rows // NA_GROUP, group, 0)


def _natten_bias_table(rpb, rows):
    kh = min(NA_KH, rows)
    r = np.arange(rows)
    rs = np.clip(r - kh // 2, 0, rows - kh)
    n_case = int((r - rs).max()) + 1
    case = np.arange(n_case)
    dr = np.arange(kh)[None, :] - case[:, None] + (NA_KH - 1)
    c = np.arange(GRID_W)
    col_start = np.clip(c - NA_KW // 2, 0, GRID_W - NA_KW)
    col_mask = (c[None, :] >= col_start[:, None]) & (c[None, :] < col_start[:, None] + NA_KW)
    dc = np.clip(c[None, :] - c[:, None], -(NA_KW - 1), NA_KW - 1) + (NA_KW - 1)
    depth, heads, _, n_dc = rpb.shape
    by_row = jnp.take(rpb.astype(F32).reshape(depth * heads, -1, n_dc), dr.reshape(-1), axis=1)
    by_row = by_row.reshape(depth * heads, n_case, kh, n_dc)
    pick = ((dc[None, :, :] == np.arange(n_dc)[:, None, None]) & col_mask[None]).astype(np.float32)
    masked = np.where(col_mask, 0.0, MASK_NEG).astype(np.float32)
    tab = jnp.einsum("xcjd,dqk->xcqjk", by_row, pick, precision=lax.Precision.HIGHEST)
    tab = tab + masked[None, None, :, None, :]
    return tab.reshape(depth, heads, n_case, GRID_W, kh * GRID_W)


def _natten(z3, bias_tab, layer, batch, seq):
    n_case, win = bias_tab.shape[2], bias_tab.shape[4]
    blk = lambda cb0: pl.BlockSpec((None, seq, LANES), lambda b, p: (cb0 + p, b, 0))
    return pl.pallas_call(
        _natten_kernel,
        grid=(batch, NA_HEADS // 2),
        in_specs=[
            blk(CB_NQ), blk(CB_NK), blk(CB_NV),
            pl.BlockSpec((None, 2, n_case, GRID_W, win), lambda b, p: (layer, p, 0, 0, 0)),
        ],
        out_specs=pl.BlockSpec((None, seq, LANES), lambda b, p: (p, b, 0)),
        out_shape=jax.ShapeDtypeStruct((NA_HEADS // 2, batch * seq, LANES), BF16),
        compiler_params=_cparams("parallel", "parallel"),
        name="natten",
    )(z3, z3, z3, bias_tab)


def _memattn_kernel(q_ref, kv_ref, o_ref):
    nt = (((1,), (1,)), ((), ()))
    for h in range(CA_HEADS):
        s = lax.dot_general(q_ref[h], kv_ref[h], nt, preferred_element_type=F32) * (CA_DIM ** -0.5)
        m = jnp.max(s, axis=-1, keepdims=True)
        p = jnp.exp(s - m)
        l = jnp.sum(p, axis=-1, keepdims=True)
        o = jnp.dot(p.astype(BF16), kv_ref[CA_HEADS + h], preferred_element_type=F32) / l
        o_ref[h] = o.astype(o_ref.dtype)


def _memattn(z3, kv4, layer, batch, seq, mem_len, tq):
    nq = seq // tq
    return pl.pallas_call(
        _memattn_kernel,
        grid=(batch, nq),
        in_specs=[
            pl.BlockSpec((CA_HEADS, tq, LANES), lambda b, i: (CB_CQ // CA_HEADS, b * nq + i, 0)),
            pl.BlockSpec((None, 2 * CA_HEADS, mem_len, LANES), lambda b, i: (layer, 0, b, 0)),
        ],
        out_specs=pl.BlockSpec((CA_HEADS, tq, LANES), lambda b, i: (0, b * nq + i, 0)),
        out_shape=jax.ShapeDtypeStruct((CA_HEADS, batch * seq, LANES), BF16),
        compiler_params=_cparams("parallel", "parallel"),
        name="memattn",
    )(z3, kv4)


def _cat(ref):
    return jnp.concatenate([ref[j] for j in range(ref.shape[0])], axis=-1)


def _merge_kernel(x_ref, ohg_ref, ona_ref, oca_ref, ghg_ref, gna_ref, gca_ref,
                  whg_ref, wna_ref, wca_ref, wout_ref, o_ref):
    def branch(o3_ref, w_ref, g_ref):
        y = jnp.dot(_cat(o3_ref), w_ref[...], preferred_element_type=F32)
        return _sigmoid(_cat(g_ref).astype(F32)) * y

    merged = branch(ohg_ref, whg_ref, ghg_ref)
    merged = merged + branch(ona_ref, wna_ref, gna_ref)
    merged = merged + branch(oca_ref, wca_ref, gca_ref)
    o_ref[...] = x_ref[...] + jnp.dot(merged.astype(BF16), wout_ref[...],
                                      preferred_element_type=F32)


def _merge(x2, ohg3, ona3, oca3, z3, w_hg_o, w_na_o, w_ca_o, w_out, layer, tm):
    m, d = x2.shape
    ncb = d // LANES
    act = lambda n: pl.BlockSpec((n, tm, LANES), lambda i: (0, i, 0))
    gate = lambda cb0: pl.BlockSpec((ncb, tm, LANES), lambda i: (cb0 // ncb, i, 0))
    full = lambda a: pl.BlockSpec((None,) + a.shape[1:], lambda i: (layer, 0, 0),
                                  pipeline_mode=pl.Buffered(1))
    return pl.pallas_call(
        _merge_kernel,
        grid=(m // tm,),
        in_specs=[
            pl.BlockSpec((tm, d), lambda i: (i, 0)),
            act(ohg3.shape[0]), act(ona3.shape[0]), act(oca3.shape[0]),
            gate(CB_GHG), gate(CB_GNA), gate(CB_GCA),
            full(w_hg_o), full(w_na_o), full(w_ca_o), full(w_out),
        ],
        out_specs=pl.BlockSpec((tm, d), lambda i: (i, 0)),
        out_shape=jax.ShapeDtypeStruct((m, d), F32),
        compiler_params=_cparams("parallel"),
        name="merge",
    )(x2, ohg3, ona3, oca3, z3, z3, z3, w_hg_o, w_na_o, w_ca_o, w_out)


def _ffn_kernel(x_ref, xp_ref, xn_ref, gain_ref, wup_ref, cw_ref, cb_ref, wd_ref, fin_ref, o_ref,
                h_ref, y_ref, *, tiles_per_seq, final_norm, ts):
    i = pl.program_id(0)
    tm = x_ref.shape[0]
    halo = SUBLANES

    def normed(x):
        ms = jnp.mean(x * x, axis=-1, keepdims=True)
        return x * lax.rsqrt(ms + EPS) * gain_ref[...]

    keep_prev = (i % tiles_per_seq != 0).astype(F32)
    keep_next = (i % tiles_per_seq != tiles_per_seq - 1).astype(F32)
    h_ref[0:halo, :] = (normed(xp_ref[...]) * keep_prev).astype(BF16)
    h_ref[halo:halo + tm, :] = normed(x_ref[...]).astype(BF16)
    h_ref[halo + tm:, :] = (normed(xn_ref[...]) * keep_next).astype(BF16)

    def conv_proj(cols):
        u = jnp.dot(h_ref[...], wup_ref[:, cols], preferred_element_type=F32)
        rows = u.shape[0]
        prev = pltpu.roll(u, 1, 0)[halo:halo + tm]
        nxt = pltpu.roll(u, rows - 1, 0)[halo:halo + tm]
        out = prev * cw_ref[0:1, cols]
        out = out + u[halo:halo + tm] * cw_ref[1:2, cols]
        out = out + nxt * cw_ref[2:3, cols]
        return out + cb_ref[:, cols]

    def sub_tile(s, carry):
        cols_a = pl.ds(pl.multiple_of(s * ts, ts), ts)
        cols_g = pl.ds(pl.multiple_of(D_FF + s * ts, ts), ts)
        a = conv_proj(cols_a)
        g = conv_proj(cols_g)
        c = np.float32(np.sqrt(2 / np.pi))
        th = jnp.tanh(a * ((a * a) * (c * np.float32(0.044715)) + c))
        half_ag = (a * g) * 0.5
        y_ref[:, cols_a] = (half_ag + half_ag * th).astype(BF16)
        return carry

    lax.fori_loop(0, D_FF // ts, sub_tile, 0, unroll=True)

    out = x_ref[...] + jnp.dot(y_ref[...], wd_ref[...], preferred_element_type=F32)
    if final_norm:
        ms = jnp.mean(out * out, axis=-1, keepdims=True)
        out = out * lax.rsqrt(ms + EPS) * fin_ref[...]
    o_ref[...] = out


def _conv_ffn(x2, gain3, w_up, conv_w, conv_b3, w_down, fin_gain, layer, seq, tm, ts, final_norm):
    m, d = x2.shape
    hb = tm // SUBLANES
    n_halo = m // SUBLANES
    kern = functools.partial(_ffn_kernel, tiles_per_seq=seq // tm, final_norm=final_norm, ts=ts)
    resident = lambda a: pl.BlockSpec((None,) + a.shape[1:], lambda i: (layer, 0, 0),
                                      pipeline_mode=pl.Buffered(1))
    return pl.pallas_call(
        kern,
        grid=(m // tm,),
        in_specs=[
            pl.BlockSpec((tm, d), lambda i: (i, 0)),
            pl.BlockSpec((SUBLANES, d), lambda i: (jnp.maximum(i * hb - 1, 0), 0)),
            pl.BlockSpec((SUBLANES, d), lambda i: (jnp.minimum((i + 1) * hb, n_halo - 1), 0)),
            pl.BlockSpec((None, 1, d), lambda i: (layer, 0, 0)),
            resident(w_up), resident(conv_w), resident(conv_b3), resident(w_down),
            pl.BlockSpec((1, d), lambda i: (0, 0)),
        ],
        out_specs=pl.BlockSpec((tm, d), lambda i: (i, 0)),
        out_shape=jax.ShapeDtypeStruct((m, d), F32),
        scratch_shapes=[
            pltpu.VMEM((tm + 2 * SUBLANES, d), BF16),
            pltpu.VMEM((tm, D_FF), BF16),
        ],
        compiler_params=_cparams("parallel"),
        name="conv_ffn",
    )(x2, x2, x2, gain3, w_up, conv_w, conv_b3, w_down, fin_gain.reshape(1, d))


def kernel(x, mem, norm_mix, w_in, hg_lb_logits, hg_gnorm, w_hg_o, na_rpb, w_na_o, mem_norm,
           w_mem_kv, w_ca_o, w_out, norm_ffn, w_up, conv_w, conv_b, w_down, norm_final):
    batch, seq, d = x.shape
    mem_len = mem.shape[1]
    depth = w_in.shape[0]
    assert d == D_MODEL and seq % GRID_W == 0 and seq % HG_CHUNK == 0
    hgw = HG_HEADS * HG_DIM

    p_lb = jax.nn.softmax(hg_lb_logits.astype(F32), axis=0)
    lower_bounds = jnp.clip(jnp.cumsum(p_lb, axis=0) - p_lb[0], 0.0, 1.0)

    bf = lambda a: a.astype(BF16)
    w_in, w_mem_kv, w_hg_o, w_na_o, w_ca_o, w_out, w_up, w_down = map(
        bf, (w_in, w_mem_kv, w_hg_o, w_na_o, w_ca_o, w_out, w_up, w_down))
    vec3 = lambda a: a.reshape(depth, 1, -1)
    norm_mix3, norm_ffn3, gnorm3, conv_b3 = map(vec3, (norm_mix, norm_ffn, hg_gnorm, conv_b))
    lb3 = vec3(lower_bounds)
    bias_tab = _natten_bias_table(na_rpb, seq // GRID_W)

    assert w_in.shape[-1] == (N_CB + 2 * hgw // LANES) * LANES

    x2 = x.reshape(batch * seq, d)
    kv4 = _mem_kv_proj(mem.reshape(batch * mem_len, d), mem_norm, w_mem_kv)

    for l in range(depth):
        z3, k3 = _in_proj(x2, norm_mix3, w_in, lb3, l, tm=ROW_TILE, tn=2 * hgw)

        ohg3 = _hgrn2(z3, k3, gnorm3, l, batch, seq)
        ona3 = _natten(z3, bias_tab, l, batch, seq)
        oca3 = _memattn(z3, kv4, l, batch, seq, mem_len, tq=MEMATTN_ROWS)

        x2 = _merge(x2, ohg3, ona3, oca3, z3, w_hg_o, w_na_o, w_ca_o, w_out, l, tm=ROW_TILE)
        x2 = _conv_ffn(x2, norm_ffn3, w_up, conv_w, conv_b3, w_down, norm_final, l, seq,
                       tm=ROW_TILE, ts=FFN_SUB, final_norm=(l == depth - 1))
    return x2.reshape(batch, seq, d)
```

```python
import functools

import numpy as np
import jax
import jax.numpy as jnp
from jax import lax
from jax.experimental import pallas as pl
from jax.experimental.pallas import tpu as pltpu

D_MODEL = 1024
GRID_W = 64
HG_HEADS = 8
HG_DIM = 128
NA_HEADS = 8
NA_DIM = 64
NA_KH = 8
NA_KW = 16
CA_HEADS = 4
CA_DIM = 128
D_FF = 2816
CONV_W = 3
EPS = 1e-6
F_FLOOR = 1e-12
MASK_NEG = -1e30

LANES = 128
SUBLANES = 8
VMEM_LIMIT = 56 * 1024 * 1024

ROW_TILE = 1024
FFN_SUB = 256
MEMATTN_ROWS = 512
HG_CHUNK = 128
HG_UNROLL = 8
NA_GROUP = 8

CB_HQ, CB_HI, CB_HG = 0, 8, 16
CB_NQ, CB_NK, CB_NV = 24, 28, 32
CB_CQ = 36
CB_GHG, CB_GNA, CB_GCA = 40, 48, 56
N_CB = 64

F32 = jnp.float32
BF16 = jnp.bfloat16


def _cparams(*sem):
    return pltpu.CompilerParams(dimension_semantics=sem, vmem_limit_bytes=VMEM_LIMIT)


def _sigmoid(x):
    return 1.0 / (1.0 + jnp.exp(-x))


PROJ_SUB = 512


def _normalise_rows(x_ref, g_ref, h_ref, weight_axis=1):
    @pl.when(pl.program_id(weight_axis) == 0)
    def _():
        x = x_ref[...]
        ms = jnp.mean(x * x, axis=-1, keepdims=True)
        h_ref[...] = (x * lax.rsqrt(ms + EPS) * g_ref[...]).astype(BF16)


def _sub_tiles(h_ref, w_ref):
    for s in range(w_ref.shape[1] // PROJ_SUB):
        acc = jnp.dot(h_ref[...], w_ref[:, s * PROJ_SUB:(s + 1) * PROJ_SUB],
                      preferred_element_type=F32)
        yield s * (PROJ_SUB // LANES), acc


def _mem_kv_kernel(x_ref, g_ref, w_ref, o_ref, h_ref):
    _normalise_rows(x_ref, g_ref, h_ref, weight_axis=0)
    for cb0, acc in _sub_tiles(h_ref, w_ref):
        for jj in range(PROJ_SUB // LANES):
            o_ref[cb0 + jj] = acc[:, jj * LANES:(jj + 1) * LANES].astype(o_ref.dtype)


def _mem_kv_proj(mem2, gain, w_bf16):
    m, d = mem2.shape
    depth, _, n = w_bf16.shape
    return pl.pallas_call(
        _mem_kv_kernel,
        grid=(depth,),
        in_specs=[
            pl.BlockSpec((m, d), lambda l: (0, 0)),
            pl.BlockSpec((1, d), lambda l: (0, 0)),
            pl.BlockSpec((None, d, n), lambda l: (l, 0, 0)),
        ],
        out_specs=pl.BlockSpec((None, n // LANES, m, LANES), lambda l: (l, 0, 0, 0)),
        out_shape=jax.ShapeDtypeStruct((depth, n // LANES, m, LANES), BF16),
        scratch_shapes=[pltpu.VMEM((m, d), BF16)],
        compiler_params=_cparams("arbitrary"),
        name="mem_kv_proj",
    )(mem2, gain.reshape(1, d), w_bf16)


IN_TILE_Q, IN_TILE_FORGET, IN_TILE_GATE = 0, 1, 2


def _in_proj_kernel(x_ref, g_ref, w_ref, lb_ref, z_ref, k_ref, h_ref):
    _normalise_rows(x_ref, g_ref, h_ref)
    j = pl.program_id(1)
    half = z_ref.shape[0] // 2

    has_silu = jnp.logical_or(j == IN_TILE_Q, j == IN_TILE_GATE)

    @pl.when(has_silu)
    def _():
        scale = jnp.where(j == IN_TILE_Q, HG_DIM ** -0.5, 1.0)
        for cb0, acc in _sub_tiles(h_ref, w_ref):
            for jj in range(PROJ_SUB // LANES):
                z = acc[:, jj * LANES:(jj + 1) * LANES]
                if cb0 + jj < half:
                    z = z * _sigmoid(z) * scale
                z_ref[cb0 + jj] = z.astype(z_ref.dtype)

    @pl.when(j > IN_TILE_GATE)
    def _():
        for cb0, acc in _sub_tiles(h_ref, w_ref):
            for jj in range(PROJ_SUB // LANES):
                z_ref[cb0 + jj] = acc[:, jj * LANES:(jj + 1) * LANES].astype(z_ref.dtype)

    @pl.when(j == IN_TILE_FORGET)
    def _():
        for cb0, acc in _sub_tiles(h_ref, w_ref):
            for jj in range(PROJ_SUB // LANES):
                z = acc[:, jj * LANES:(jj + 1) * LANES]
                cb = cb0 + jj
                t = jnp.exp(-jnp.abs(z))
                r = 1.0 / (1.0 + t)
                sig_neg = jnp.where(z >= 0.0, t * r, r)
                k_ref[cb] = (1.0 - lb_ref[:, cb * LANES:(cb + 1) * LANES]) * sig_neg


def _in_proj(x2, gain3, w_bf16, lb3, layer, tm, tn):
    m, d = x2.shape
    n_tiles = w_bf16.shape[-1] // tn
    ncb = tn // LANES
    z_tile = lambda j: j - (j >= IN_TILE_FORGET).astype(jnp.int32)
    return pl.pallas_call(
        _in_proj_kernel,
        grid=(m // tm, n_tiles),
        in_specs=[
            pl.BlockSpec((tm, d), lambda i, j: (i, 0)),
            pl.BlockSpec((None, 1, d), lambda i, j: (layer, 0, 0)),
            pl.BlockSpec((None, d, tn), lambda i, j: (layer, 0, j)),
            pl.BlockSpec((None, 1, tn), lambda i, j: (layer, 0, 0)),
        ],
        out_specs=[pl.BlockSpec((ncb, tm, LANES), lambda i, j: (z_tile(j), i, 0)),
                   pl.BlockSpec((ncb, tm, LANES), lambda i, j: (0, i, 0))],
        out_shape=[jax.ShapeDtypeStruct(((n_tiles - 1) * ncb, m, LANES), BF16),
                   jax.ShapeDtypeStruct((ncb, m, LANES), F32)],
        scratch_shapes=[pltpu.VMEM((tm, d), BF16)],
        compiler_params=_cparams("parallel", "arbitrary"),
        name="in_proj",
    )(x2, gain3, w_bf16, lb3)


def _slabs(x):
    return [x[i:i + SUBLANES, :] for i in range(0, x.shape[0], SUBLANES)]


def _join(slabs):
    return jnp.concatenate(slabs, axis=0)


def _pair_scores(q_slabs, k_slabs):
    k_t = jnp.transpose(_join(k_slabs)).astype(BF16)
    return jnp.dot(_join(q_slabs).astype(BF16), k_t, preferred_element_type=F32)


def _double_blocks(qe, kx, tot, b, row8, reverse):
    n = len(qe)
    if b < SUBLANES:
        later = (row8 & b) != 0
        if reverse:
            later = jnp.logical_not(later)
        back, fwd = (SUBLANES - b, b) if reverse else (b, SUBLANES - b)
        prev = [pltpu.roll(t, back, 0) for t in tot]
        nxt = prev if 2 * b == SUBLANES else [pltpu.roll(t, fwd, 0) for t in tot]
        qe = [qe[i] * jnp.where(later, prev[i], 1.0) for i in range(n)]
        kx = [kx[i] * jnp.where(later, 1.0, nxt[i]) for i in range(n)]
        if 2 * b == SUBLANES:
            tot = [tot[i] * prev[i] for i in range(n)]
        else:
            tot = [tot[i] * jnp.where(later, prev[i], nxt[i]) for i in range(n)]
        return qe, kx, tot
    m = b // SUBLANES
    qe, kx, merged = list(qe), list(kx), []
    for p in range(0, n, 2 * m):
        first, second = (p + m, p) if reverse else (p, p + m)
        t_first, t_second = tot[first // m], tot[second // m]
        for i in range(m):
            qe[second + i] = qe[second + i] * t_first
            kx[first + i] = kx[first + i] * t_second
        merged.append(t_first * t_second)
    return qe, kx, merged


def _hgrn2_kernel(q_ref, v_ref, g_ref, kf_ref, kb_ref, gn_ref, o_ref,
                  oi_ref, qd_ref, kv_ref, df_ref, db_ref, st_ref):
    c = HG_CHUNK
    nc = q_ref.shape[0] // c
    nt = (((1,), (1,)), ((), ()))
    tn = (((0,), (0,)), ((), ()))
    row8 = lax.broadcasted_iota(jnp.int32, (SUBLANES, LANES), 0)
    pair_xor = (lax.broadcasted_iota(jnp.int32, (c, c), 0)
                ^ lax.broadcasted_iota(jnp.int32, (c, c), 1))
    n_slab = c // SUBLANES

    def intra(ci, carry):
        rows = pl.ds(pl.multiple_of(ci * c, c), c)
        q = _slabs(q_ref[rows, :].astype(F32))
        v = v_ref[rows, :]
        k_f = _slabs(kf_ref[rows, :])
        k_b = _slabs(kb_ref[rows, :])
        f_f = [jnp.maximum(1.0 - k, F_FLOOR) for k in k_f]
        f_b = [jnp.maximum(1.0 - k, F_FLOOR) for k in k_b]
        fwd = ([q[i] * f_f[i] for i in range(n_slab)], k_f, f_f)
        bwd = ([q[i] * f_b[i] for i in range(n_slab)], k_b, f_b)

        k_both = [k_f[i] + k_b[i] for i in range(n_slab)]
        scores = _pair_scores(q, k_both)
        b = 1
        while b < c:
            (q_f, x_f, _), (q_b, x_b, _) = fwd, bwd
            if b >= SUBLANES:
                is_right = [(i // (b // SUBLANES)) % 2 == 1 for i in range(n_slab)]
                qh = [q_f[i] if is_right[i] else q_b[i] for i in range(n_slab)]
                kh = [x_b[i] if is_right[i] else x_f[i] for i in range(n_slab)]
            else:
                right = (row8 & b) != 0
                qh = [jnp.where(right, q_f[i], q_b[i]) for i in range(n_slab)]
                kh = [jnp.where(right, x_b[i], x_f[i]) for i in range(n_slab)]
            r = _pair_scores(qh, kh)
            scores = jnp.where(pair_xor >= b, r, scores)
            fwd = _double_blocks(*fwd, b, row8, False)
            bwd = _double_blocks(*bwd, b, row8, True)
            b *= 2
        oi_ref[rows, :] = jnp.dot(scores.astype(BF16), v, preferred_element_type=F32)

        (q_f, x_f, tot_f), (q_b, x_b, tot_b) = fwd, bwd
        qd_ref[rows, 0:LANES] = _join(q_f).astype(BF16)
        qd_ref[rows, LANES:] = _join(q_b).astype(BF16)
        k_dec = jnp.concatenate([_join(x_f).astype(BF16), _join(x_b).astype(BF16)], axis=1)
        kv_ref[ci] = lax.dot_general(v, k_dec, tn, preferred_element_type=F32)
        df_ref[ci] = tot_f[0]
        db_ref[ci] = tot_b[0]
        return carry

    lax.fori_loop(0, nc, intra, 0, unroll=HG_UNROLL)

    def scan_states(i, carry):
        s_f, s_b = carry
        cb = nc - 1 - i
        st_ref[i, :, 0:LANES] = s_f.astype(BF16)
        st_ref[cb, :, LANES:] = s_b.astype(BF16)
        s_f = s_f * df_ref[i][0:1, :] + kv_ref[i, :, 0:LANES]
        s_b = s_b * db_ref[cb][0:1, :] + kv_ref[cb, :, LANES:]
        return s_f, s_b

    zero = jnp.zeros((HG_DIM, HG_DIM), F32)
    lax.fori_loop(0, nc, scan_states, (zero, zero))

    gn = gn_ref[...]

    def finish(ci, carry):
        rows = pl.ds(pl.multiple_of(ci * c, c), c)
        o = oi_ref[rows, :]
        o = o + lax.dot_general(qd_ref[rows, :], st_ref[ci], nt, preferred_element_type=F32)
        ms = jnp.mean(o * o, axis=-1, keepdims=True)
        o = o * lax.rsqrt(ms + EPS) * gn
        o_ref[rows, :] = (o * g_ref[rows, :].astype(F32)).astype(o_ref.dtype)
        return carry

    lax.fori_loop(0, nc, finish, 0, unroll=HG_UNROLL)


def _hgrn2(z3, k3, gnorm3, layer, batch, seq):
    c = HG_CHUNK
    nc = seq // c
    blk = lambda cb0: pl.BlockSpec((None, seq, LANES), lambda b, h: (cb0 + h, b, 0))
    return pl.pallas_call(
        _hgrn2_kernel,
        grid=(batch, HG_HEADS),
        in_specs=[
            blk(CB_HQ), blk(CB_HI), blk(CB_HG),
            blk(0), blk(HG_HEADS),
            pl.BlockSpec((None, 1, LANES), lambda b, h: (layer, 0, 0)),
        ],
        out_specs=pl.BlockSpec((None, seq, LANES), lambda b, h: (h, b, 0)),
        out_shape=jax.ShapeDtypeStruct((HG_HEADS, batch * seq, LANES), BF16),
        scratch_shapes=[
            pltpu.VMEM((seq, LANES), F32),
            pltpu.VMEM((seq, 2 * LANES), BF16),
            pltpu.VMEM((nc, HG_DIM, 2 * HG_DIM), F32),
            pltpu.VMEM((nc, SUBLANES, LANES), F32),
            pltpu.VMEM((nc, SUBLANES, LANES), F32),
            pltpu.VMEM((nc, HG_DIM, 2 * HG_DIM), BF16),
        ],
        compiler_params=_cparams("parallel", "parallel"),
        name="hgrn2",
    )(z3, z3, z3, k3, k3, gnorm3)


def _natten_kernel(q_ref, k_ref, v_ref, bias_ref, o_ref):
    rows = q_ref.shape[0] // GRID_W
    kh = min(NA_KH, rows)
    win = kh * GRID_W
    nt = (((1,), (1,)), ((), ()))
    lane = lax.broadcasted_iota(jnp.int32, (GRID_W, LANES), 1)
    first = lane < NA_DIM

    def group(gi, carry):
        scored = []
        for u in range(NA_GROUP):
            r = gi * NA_GROUP + u
            rs = jnp.clip(r - kh // 2, 0, rows - kh)
            q2 = q_ref[pl.ds(pl.multiple_of(r * GRID_W, GRID_W), GRID_W), :] * (NA_DIM ** -0.5)
            zero = jnp.zeros_like(q2)
            q_st = jnp.concatenate([jnp.where(first, q2, zero), jnp.where(first, zero, q2)], axis=0)
            kw = k_ref[pl.ds(pl.multiple_of(rs * GRID_W, GRID_W), win), :]
            scored.append((lax.dot_general(q_st, kw, nt, preferred_element_type=F32), r, rs))
        probs = []
        for s, r, rs in scored:
            bm = jnp.concatenate([bias_ref[0, r - rs], bias_ref[1, r - rs]], axis=0)
            s = jnp.where(bm > 0.5 * MASK_NEG, s + bm, MASK_NEG)
            m = jnp.max(s, axis=-1, keepdims=True)
            p = jnp.exp(s - m)
            probs.append((p.astype(BF16), jnp.sum(p, axis=-1, keepdims=True), r, rs))
        for p, l, r, rs in probs:
            vw = v_ref[pl.ds(pl.multiple_of(rs * GRID_W, GRID_W), win), :]
            o_st = jnp.dot(p, vw, preferred_element_type=F32) / l
            o = jnp.where(first, o_st[:GRID_W], o_st[GRID_W:])
            o_ref[pl.ds(pl.multiple_of(r * GRID_W, GRID_W), GRID_W), :] = o.astype(o_ref.dtype)
        return carry

    lax.fori_loop(0, rows // NA_GROUP, group, 0)


def _natten_bias_table(rpb, rows):
    kh = min(NA_KH, rows)
    r = np.arange(rows)
    rs = np.clip(r - kh // 2, 0, rows - kh)
    n_case = int((r - rs).max()) + 1
    case = np.arange(n_case)
    dr = np.arange(kh)[None, :] - case[:, None] + (NA_KH - 1)
    c = np.arange(GRID_W)
    col_start = np.clip(c - NA_KW // 2, 0, GRID_W - NA_KW)
    col_mask = (c[None, :] >= col_start[:, None]) & (c[None, :] < col_start[:, None] + NA_KW)
    dc = np.clip(c[None, :] - c[:, None], -(NA_KW - 1), NA_KW - 1) + (NA_KW - 1)
    depth, heads, _, n_dc = rpb.shape
    by_row = jnp.take(rpb.astype(F32).reshape(depth * heads, -1, n_dc), dr.reshape(-1), axis=1)
    by_row = by_row.reshape(depth * heads, n_case, kh, n_dc)
    pick = ((dc[None, :, :] == np.arange(n_dc)[:, None, None]) & col_mask[None]).astype(np.float32)
    masked = np.where(col_mask, 0.0, MASK_NEG).astype(np.float32)
    tab = jnp.einsum("xcjd,dqk->xcqjk", by_row, pick, precision=lax.Precision.HIGHEST)
    tab = tab + masked[None, None, :, None, :]
    return tab.reshape(depth, heads, n_case, GRID_W, kh * GRID_W)


def _natten(z3, bias_tab, layer, batch, seq):
    n_case, win = bias_tab.shape[2], bias_tab.shape[4]
    blk = lambda cb0: pl.BlockSpec((None, seq, LANES), lambda b, p: (cb0 + p, b, 0))
    return pl.pallas_call(
        _natten_kernel,
        grid=(batch, NA_HEADS // 2),
        in_specs=[
            blk(CB_NQ), blk(CB_NK), blk(CB_NV),
            pl.BlockSpec((None, 2, n_case, GRID_W, win), lambda b, p: (layer, p, 0, 0, 0)),
        ],
        out_specs=pl.BlockSpec((None, seq, LANES), lambda b, p: (p, b, 0)),
        out_shape=jax.ShapeDtypeStruct((NA_HEADS // 2, batch * seq, LANES), BF16),
        compiler_params=_cparams("parallel", "parallel"),
        name="natten",
    )(z3, z3, z3, bias_tab)


def _memattn_kernel(q_ref, kv_ref, o_ref):
    nt = (((1,), (1,)), ((), ()))
    for h in range(CA_HEADS):
        s = lax.dot_general(q_ref[h], kv_ref[h], nt, preferred_element_type=F32) * (CA_DIM ** -0.5)
        m = jnp.max(s, axis=-1, keepdims=True)
        p = jnp.exp(s - m)
        l = jnp.sum(p, axis=-1, keepdims=True)
        o = jnp.dot(p.astype(BF16), kv_ref[CA_HEADS + h], preferred_element_type=F32) / l
        o_ref[h] = o.astype(o_ref.dtype)


def _memattn(z3, kv4, layer, batch, seq, mem_len, tq):
    nq = seq // tq
    return pl.pallas_call(
        _memattn_kernel,
        grid=(batch, nq),
        in_specs=[
            pl.BlockSpec((CA_HEADS, tq, LANES), lambda b, i: (CB_CQ // CA_HEADS, b * nq + i, 0)),
            pl.BlockSpec((None, 2 * CA_HEADS, mem_len, LANES), lambda b, i: (layer, 0, b, 0)),
        ],
        out_specs=pl.BlockSpec((CA_HEADS, tq, LANES), lambda b, i: (0, b * nq + i, 0)),
        out_shape=jax.ShapeDtypeStruct((CA_HEADS, batch * seq, LANES), BF16),
        compiler_params=_cparams("parallel", "parallel"),
        name="memattn",
    )(z3, kv4)


def _cat(ref):
    return jnp.concatenate([ref[j] for j in range(ref.shape[0])], axis=-1)


def _merge_kernel(x_ref, ohg_ref, ona_ref, oca_ref, ghg_ref, gna_ref, gca_ref,
                  whg_ref, wna_ref, wca_ref, wout_ref, o_ref):
    def branch(o3_ref, w_ref, g_ref):
        y = jnp.dot(_cat(o3_ref), w_ref[...], preferred_element_type=F32)
        return _sigmoid(_cat(g_ref).astype(F32)) * y

    merged = branch(ohg_ref, whg_ref, ghg_ref)
    merged = merged + branch(ona_ref, wna_ref, gna_ref)
    merged = merged + branch(oca_ref, wca_ref, gca_ref)
    o_ref[...] = x_ref[...] + jnp.dot(merged.astype(BF16), wout_ref[...],
                                      preferred_element_type=F32)


def _merge(x2, ohg3, ona3, oca3, z3, w_hg_o, w_na_o, w_ca_o, w_out, layer, tm):
    m, d = x2.shape
    ncb = d // LANES
    act = lambda n: pl.BlockSpec((n, tm, LANES), lambda i: (0, i, 0))
    gate = lambda cb0: pl.BlockSpec((ncb, tm, LANES), lambda i: (cb0 // ncb, i, 0))
    full = lambda a: pl.BlockSpec((None,) + a.shape[1:], lambda i: (layer, 0, 0),
                                  pipeline_mode=pl.Buffered(1))
    return pl.pallas_call(
        _merge_kernel,
        grid=(m // tm,),
        in_specs=[
            pl.BlockSpec((tm, d), lambda i: (i, 0)),
            act(ohg3.shape[0]), act(ona3.shape[0]), act(oca3.shape[0]),
            gate(CB_GHG), gate(CB_GNA), gate(CB_GCA),
            full(w_hg_o), full(w_na_o), full(w_ca_o), full(w_out),
        ],
        out_specs=pl.BlockSpec((tm, d), lambda i: (i, 0)),
        out_shape=jax.ShapeDtypeStruct((m, d), F32),
        compiler_params=_cparams("parallel"),
        name="merge",
    )(x2, ohg3, ona3, oca3, z3, z3, z3, w_hg_o, w_na_o, w_ca_o, w_out)


def _ffn_kernel(x_ref, xp_ref, xn_ref, gain_ref, wup_ref, cw_ref, cb_ref, wd_ref, fin_ref, o_ref,
                h_ref, y_ref, *, tiles_per_seq, final_norm, ts):
    i = pl.program_id(0)
    tm = x_ref.shape[0]
    halo = SUBLANES

    def normed(x):
        ms = jnp.mean(x * x, axis=-1, keepdims=True)
        return x * lax.rsqrt(ms + EPS) * gain_ref[...]

    keep_prev = (i % tiles_per_seq != 0).astype(F32)
    keep_next = (i % tiles_per_seq != tiles_per_seq - 1).astype(F32)
    h_ref[0:halo, :] = (normed(xp_ref[...]) * keep_prev).astype(BF16)
    h_ref[halo:halo + tm, :] = normed(x_ref[...]).astype(BF16)
    h_ref[halo + tm:, :] = (normed(xn_ref[...]) * keep_next).astype(BF16)

    def conv_proj(cols):
        u = jnp.dot(h_ref[...], wup_ref[:, cols], preferred_element_type=F32)
        rows = u.shape[0]
        prev = pltpu.roll(u, 1, 0)[halo:halo + tm]
        nxt = pltpu.roll(u, rows - 1, 0)[halo:halo + tm]
        out = prev * cw_ref[0:1, cols]
        out = out + u[halo:halo + tm] * cw_ref[1:2, cols]
        out = out + nxt * cw_ref[2:3, cols]
        return out + cb_ref[:, cols]

    def sub_tile(s, carry):
        cols_a = pl.ds(pl.multiple_of(s * ts, ts), ts)
        cols_g = pl.ds(pl.multiple_of(D_FF + s * ts, ts), ts)
        a = conv_proj(cols_a)
        g = conv_proj(cols_g)
        c = np.float32(np.sqrt(2 / np.pi))
        th = jnp.tanh(a * ((a * a) * (c * np.float32(0.044715)) + c))
        half_ag = (a * g) * 0.5
        y_ref[:, cols_a] = (half_ag + half_ag * th).astype(BF16)
        return carry

    lax.fori_loop(0, D_FF // ts, sub_tile, 0, unroll=True)

    out = x_ref[...] + jnp.dot(y_ref[...], wd_ref[...], preferred_element_type=F32)
    if final_norm:
        ms = jnp.mean(out * out, axis=-1, keepdims=True)
        out = out * lax.rsqrt(ms + EPS) * fin_ref[...]
    o_ref[...] = out


def _conv_ffn(x2, gain3, w_up, conv_w, conv_b3, w_down, fin_gain, layer, seq, tm, ts, final_norm):
    m, d = x2.shape
    hb = tm // SUBLANES
    n_halo = m // SUBLANES
    kern = functools.partial(_ffn_kernel, tiles_per_seq=seq // tm, final_norm=final_norm, ts=ts)
    resident = lambda a: pl.BlockSpec((None,) + a.shape[1:], lambda i: (layer, 0, 0),
                                      pipeline_mode=pl.Buffered(1))
    return pl.pallas_call(
        kern,
        grid=(m // tm,),
        in_specs=[
            pl.BlockSpec((tm, d), lambda i: (i, 0)),
            pl.BlockSpec((SUBLANES, d), lambda i: (jnp.maximum(i * hb - 1, 0), 0)),
            pl.BlockSpec((SUBLANES, d), lambda i: (jnp.minimum((i + 1) * hb, n_halo - 1), 0)),
            pl.BlockSpec((None, 1, d), lambda i: (layer, 0, 0)),
            resident(w_up), resident(conv_w), resident(conv_b3), resident(w_down),
            pl.BlockSpec((1, d), lambda i: (0, 0)),
        ],
        out_specs=pl.BlockSpec((tm, d), lambda i: (i, 0)),
        out_shape=jax.ShapeDtypeStruct((m, d), F32),
        scratch_shapes=[
            pltpu.VMEM((tm + 2 * SUBLANES, d), BF16),
            pltpu.VMEM((tm, D_FF), BF16),
        ],
        compiler_params=_cparams("parallel"),
        name="conv_ffn",
    )(x2, x2, x2, gain3, w_up, conv_w, conv_b3, w_down, fin_gain.reshape(1, d))


def kernel(x, mem, norm_mix, w_in, hg_lb_logits, hg_gnorm, w_hg_o, na_rpb, w_na_o, mem_norm,
           w_mem_kv, w_ca_o, w_out, norm_ffn, w_up, conv_w, conv_b, w_down, norm_final):
    batch, seq, d = x.shape
    mem_len = mem.shape[1]
    depth = w_in.shape[0]
    assert d == D_MODEL and seq % GRID_W == 0 and seq % HG_CHUNK == 0
    hgw = HG_HEADS * HG_DIM

    p_lb = jax.nn.softmax(hg_lb_logits.astype(F32), axis=0)
    lower_bounds = jnp.clip(jnp.cumsum(p_lb, axis=0) - p_lb[0], 0.0, 1.0)

    bf = lambda a: a.astype(BF16)
    w_in, w_mem_kv, w_hg_o, w_na_o, w_ca_o, w_out, w_up, w_down = map(
        bf, (w_in, w_mem_kv, w_hg_o, w_na_o, w_ca_o, w_out, w_up, w_down))
    vec3 = lambda a: a.reshape(depth, 1, -1)
    norm_mix3, norm_ffn3, gnorm3, conv_b3 = map(vec3, (norm_mix, norm_ffn, hg_gnorm, conv_b))
    lb3 = vec3(lower_bounds)
    bias_tab = _natten_bias_table(na_rpb, seq // GRID_W)

    assert w_in.shape[-1] == (N_CB + 2 * hgw // LANES) * LANES

    x2 = x.reshape(batch * seq, d)
    kv4 = _mem_kv_proj(mem.reshape(batch * mem_len, d), mem_norm, w_mem_kv)

    for l in range(depth):
        z3, k3 = _in_proj(x2, norm_mix3, w_in, lb3, l, tm=ROW_TILE, tn=2 * hgw)

        ohg3 = _hgrn2(z3, k3, gnorm3, l, batch, seq)
        ona3 = _natten(z3, bias_tab, l, batch, seq)
        oca3 = _memattn(z3, kv4, l, batch, seq, mem_len, tq=MEMATTN_ROWS)

        x2 = _merge(x2, ohg3, ona3, oca3, z3, w_hg_o, w_na_o, w_ca_o, w_out, l, tm=ROW_TILE)
        x2 = _conv_ffn(x2, norm_ffn3, w_up, conv_w, conv_b3, w_down, norm_final, l, seq,
                       tm=ROW_TILE, ts=FFN_SUB, final_norm=(l == depth - 1))
    return x2.reshape(batch, seq, d)
```

```python
import functools

import numpy as np
import jax
import jax.numpy as jnp
from jax import lax
from jax.experimental import pallas as pl
from jax.experimental.pallas import tpu as pltpu

D_MODEL = 1024
GRID_W = 64
HG_HEADS = 8
HG_DIM = 128
NA_HEADS = 8
NA_DIM = 64
NA_KH = 8
NA_KW = 16
CA_HEADS = 4
CA_DIM = 128
D_FF = 2816
CONV_W = 3
EPS = 1e-6
F_FLOOR = 1e-12
MASK_NEG = -1e30

LANES = 128
SUBLANES = 8
VMEM_LIMIT = 56 * 1024 * 1024

ROW_TILE = 1024
FFN_SUB = 256
HG_CHUNK = 128
HG_UNROLL = 8
NA_GROUP = 8

CB_HQ, CB_HI, CB_HG = 0, 8, 16
CB_NQ, CB_NK, CB_NV = 24, 28, 32
CB_CQ = 36
CB_GHG, CB_GNA, CB_GCA = 40, 48, 56
N_CB = 64

F32 = jnp.float32
BF16 = jnp.bfloat16


def _cparams(*sem):
    return pltpu.CompilerParams(dimension_semantics=sem, vmem_limit_bytes=VMEM_LIMIT)


def _sigmoid(x):
    return 1.0 / (1.0 + jnp.exp(-x))


PROJ_SUB = 512


def _normalise_rows(x_ref, g_ref, h_ref, weight_axis=1):
    @pl.when(pl.program_id(weight_axis) == 0)
    def _():
        x = x_ref[...]
        ms = jnp.mean(x * x, axis=-1, keepdims=True)
        h_ref[...] = (x * lax.rsqrt(ms + EPS) * g_ref[...]).astype(BF16)


def _sub_tiles(h_ref, w_ref):
    for s in range(w_ref.shape[1] // PROJ_SUB):
        acc = jnp.dot(h_ref[...], w_ref[:, s * PROJ_SUB:(s + 1) * PROJ_SUB],
                      preferred_element_type=F32)
        yield s * (PROJ_SUB // LANES), acc


def _mem_kv_kernel(x_ref, g_ref, w_ref, o_ref, h_ref):
    _normalise_rows(x_ref, g_ref, h_ref, weight_axis=0)
    for cb0, acc in _sub_tiles(h_ref, w_ref):
        for jj in range(PROJ_SUB // LANES):
            o_ref[cb0 + jj] = acc[:, jj * LANES:(jj + 1) * LANES].astype(o_ref.dtype)


def _mem_kv_proj(mem2, gain, w_bf16):
    m, d = mem2.shape
    depth, _, n = w_bf16.shape
    return pl.pallas_call(
        _mem_kv_kernel,
        grid=(depth,),
        in_specs=[
            pl.BlockSpec((m, d), lambda l: (0, 0)),
            pl.BlockSpec((1, d), lambda l: (0, 0)),
            pl.BlockSpec((None, d, n), lambda l: (l, 0, 0)),
        ],
        out_specs=pl.BlockSpec((None, n // LANES, m, LANES), lambda l: (l, 0, 0, 0)),
        out_shape=jax.ShapeDtypeStruct((depth, n // LANES, m, LANES), BF16),
        scratch_shapes=[pltpu.VMEM((m, d), BF16)],
        compiler_params=_cparams("arbitrary"),
        name="mem_kv_proj",
    )(mem2, gain.reshape(1, d), w_bf16)


IN_TILE_Q, IN_TILE_FORGET, IN_TILE_GATE = 0, 1, 2


def _in_proj_kernel(x_ref, g_ref, w_ref, lb_ref, z_ref, k_ref, h_ref):
    _normalise_rows(x_ref, g_ref, h_ref)
    j = pl.program_id(1)
    half = z_ref.shape[0] // 2

    has_silu = jnp.logical_or(j == IN_TILE_Q, j == IN_TILE_GATE)

    @pl.when(has_silu)
    def _():
        scale = jnp.where(j == IN_TILE_Q, HG_DIM ** -0.5, 1.0)
        for cb0, acc in _sub_tiles(h_ref, w_ref):
            for jj in range(PROJ_SUB // LANES):
                z = acc[:, jj * LANES:(jj + 1) * LANES]
                if cb0 + jj < half:
                    z = z * _sigmoid(z) * scale
                z_ref[cb0 + jj] = z.astype(z_ref.dtype)

    @pl.when(j > IN_TILE_GATE)
    def _():
        for cb0, acc in _sub_tiles(h_ref, w_ref):
            for jj in range(PROJ_SUB // LANES):
                z_ref[cb0 + jj] = acc[:, jj * LANES:(jj + 1) * LANES].astype(z_ref.dtype)

    @pl.when(j == IN_TILE_FORGET)
    def _():
        for cb0, acc in _sub_tiles(h_ref, w_ref):
            for jj in range(PROJ_SUB // LANES):
                z = acc[:, jj * LANES:(jj + 1) * LANES]
                cb = cb0 + jj
                t = jnp.exp(-jnp.abs(z))
                r = 1.0 / (1.0 + t)
                sig_neg = jnp.where(z >= 0.0, t * r, r)
                k_ref[cb] = (1.0 - lb_ref[:, cb * LANES:(cb + 1) * LANES]) * sig_neg


def _in_proj(x2, gain3, w_bf16, lb3, layer, tm, tn):
    m, d = x2.shape
    n_tiles = w_bf16.shape[-1] // tn
    ncb = tn // LANES
    z_tile = lambda j: j - (j >= IN_TILE_FORGET).astype(jnp.int32)
    return pl.pallas_call(
        _in_proj_kernel,
        grid=(m // tm, n_tiles),
        in_specs=[
            pl.BlockSpec((tm, d), lambda i, j: (i, 0)),
            pl.BlockSpec((None, 1, d), lambda i, j: (layer, 0, 0)),
            pl.BlockSpec((None, d, tn), lambda i, j: (layer, 0, j)),
            pl.BlockSpec((None, 1, tn), lambda i, j: (layer, 0, 0)),
        ],
        out_specs=[pl.BlockSpec((ncb, tm, LANES), lambda i, j: (z_tile(j), i, 0)),
                   pl.BlockSpec((ncb, tm, LANES), lambda i, j: (0, i, 0))],
        out_shape=[jax.ShapeDtypeStruct(((n_tiles - 1) * ncb, m, LANES), BF16),
                   jax.ShapeDtypeStruct((ncb, m, LANES), F32)],
        scratch_shapes=[pltpu.VMEM((tm, d), BF16)],
        compiler_params=_cparams("parallel", "arbitrary"),
        name="in_proj",
    )(x2, gain3, w_bf16, lb3)


def _slabs(x):
    return [x[i:i + SUBLANES, :] for i in range(0, x.shape[0], SUBLANES)]


def _join(slabs):
    return jnp.concatenate(slabs, axis=0)


def _pair_scores(q_slabs, k_slabs):
    k_t = jnp.transpose(_join(k_slabs)).astype(BF16)
    return jnp.dot(_join(q_slabs).astype(BF16), k_t, preferred_element_type=F32)


def _double_blocks(qe, kx, tot, b, row8, reverse):
    n = len(qe)
    if b < SUBLANES:
        later = (row8 & b) != 0
        if reverse:
            later = jnp.logical_not(later)
        back, fwd = (SUBLANES - b, b) if reverse else (b, SUBLANES - b)
        prev = [pltpu.roll(t, back, 0) for t in tot]
        nxt = prev if 2 * b == SUBLANES else [pltpu.roll(t, fwd, 0) for t in tot]
        qe = [qe[i] * jnp.where(later, prev[i], 1.0) for i in range(n)]
        kx = [kx[i] * jnp.where(later, 1.0, nxt[i]) for i in range(n)]
        if 2 * b == SUBLANES:
            tot = [tot[i] * prev[i] for i in range(n)]
        else:
            tot = [tot[i] * jnp.where(later, prev[i], nxt[i]) for i in range(n)]
        return qe, kx, tot
    m = b // SUBLANES
    qe, kx, merged = list(qe), list(kx), []
    for p in range(0, n, 2 * m):
        first, second = (p + m, p) if reverse else (p, p + m)
        t_first, t_second = tot[first // m], tot[second // m]
        for i in range(m):
            qe[second + i] = qe[second + i] * t_first
            kx[first + i] = kx[first + i] * t_second
        merged.append(t_first * t_second)
    return qe, kx, merged


def _hgrn2_kernel(q_ref, v_ref, g_ref, kf_ref, kb_ref, gn_ref, o_ref,
                  oi_ref, qd_ref, kv_ref, df_ref, db_ref, st_ref):
    c = HG_CHUNK
    nc = q_ref.shape[0] // c
    nt = (((1,), (1,)), ((), ()))
    tn = (((0,), (0,)), ((), ()))
    row8 = lax.broadcasted_iota(jnp.int32, (SUBLANES, LANES), 0)
    pair_xor = (lax.broadcasted_iota(jnp.int32, (c, c), 0)
                ^ lax.broadcasted_iota(jnp.int32, (c, c), 1))
    n_slab = c // SUBLANES

    def intra(ci, carry):
        rows = pl.ds(pl.multiple_of(ci * c, c), c)
        q = _slabs(q_ref[rows, :].astype(F32))
        v = v_ref[rows, :]
        k_f = _slabs(kf_ref[rows, :])
        k_b = _slabs(kb_ref[rows, :])
        f_f = [jnp.maximum(1.0 - k, F_FLOOR) for k in k_f]
        f_b = [jnp.maximum(1.0 - k, F_FLOOR) for k in k_b]
        fwd = ([q[i] * f_f[i] for i in range(n_slab)], k_f, f_f)
        bwd = ([q[i] * f_b[i] for i in range(n_slab)], k_b, f_b)

        k_both = [k_f[i] + k_b[i] for i in range(n_slab)]
        scores = _pair_scores(q, k_both)
        b = 1
        while b < c:
            (q_f, x_f, _), (q_b, x_b, _) = fwd, bwd
            if b >= SUBLANES:
                is_right = [(i // (b // SUBLANES)) % 2 == 1 for i in range(n_slab)]
                qh = [q_f[i] if is_right[i] else q_b[i] for i in range(n_slab)]
                kh = [x_b[i] if is_right[i] else x_f[i] for i in range(n_slab)]
            else:
                right = (row8 & b) != 0
                qh = [jnp.where(right, q_f[i], q_b[i]) for i in range(n_slab)]
                kh = [jnp.where(right, x_b[i], x_f[i]) for i in range(n_slab)]
            r = _pair_scores(qh, kh)
            scores = jnp.where(pair_xor >= b, r, scores)
            fwd = _double_blocks(*fwd, b, row8, False)
            bwd = _double_blocks(*bwd, b, row8, True)
            b *= 2
        oi_ref[rows, :] = jnp.dot(scores.astype(BF16), v, preferred_element_type=F32)

        (q_f, x_f, tot_f), (q_b, x_b, tot_b) = fwd, bwd
        qd_ref[rows, 0:LANES] = _join(q_f).astype(BF16)
        qd_ref[rows, LANES:] = _join(q_b).astype(BF16)
        k_dec = jnp.concatenate([_join(x_f).astype(BF16), _join(x_b).astype(BF16)], axis=1)
        kv_ref[ci] = lax.dot_general(v, k_dec, tn, preferred_element_type=F32)
        df_ref[ci] = tot_f[0]
        db_ref[ci] = tot_b[0]
        return carry

    lax.fori_loop(0, nc, intra, 0, unroll=HG_UNROLL)

    def scan_states(i, carry):
        s_f, s_b = carry
        cb = nc - 1 - i
        st_ref[i, :, 0:LANES] = s_f.astype(BF16)
        st_ref[cb, :, LANES:] = s_b.astype(BF16)
        s_f = s_f * df_ref[i][0:1, :] + kv_ref[i, :, 0:LANES]
        s_b = s_b * db_ref[cb][0:1, :] + kv_ref[cb, :, LANES:]
        return s_f, s_b

    zero = jnp.zeros((HG_DIM, HG_DIM), F32)
    lax.fori_loop(0, nc, scan_states, (zero, zero))

    gn = gn_ref[...]

    def finish(ci, carry):
        rows = pl.ds(pl.multiple_of(ci * c, c), c)
        o = oi_ref[rows, :]
        o = o + lax.dot_general(qd_ref[rows, :], st_ref[ci], nt, preferred_element_type=F32)
        ms = jnp.mean(o * o, axis=-1, keepdims=True)
        o = o * lax.rsqrt(ms + EPS) * gn
        o_ref[rows, :] = (o * g_ref[rows, :].astype(F32)).astype(o_ref.dtype)
        return carry

    lax.fori_loop(0, nc, finish, 0, unroll=HG_UNROLL)


def _hgrn2(z3, k3, gnorm3, layer, batch, seq):
    c = HG_CHUNK
    nc = seq // c
    blk = lambda cb0: pl.BlockSpec((None, seq, LANES), lambda b, h: (cb0 + h, b, 0))
    return pl.pallas_call(
        _hgrn2_kernel,
        grid=(batch, HG_HEADS),
        in_specs=[
            blk(CB_HQ), blk(CB_HI), blk(CB_HG),
            blk(0), blk(HG_HEADS),
            pl.BlockSpec((None, 1, LANES), lambda b, h: (layer, 0, 0)),
        ],
        out_specs=pl.BlockSpec((None, seq, LANES), lambda b, h: (h, b, 0)),
        out_shape=jax.ShapeDtypeStruct((HG_HEADS, batch * seq, LANES), BF16),
        scratch_shapes=[
            pltpu.VMEM((seq, LANES), F32),
            pltpu.VMEM((seq, 2 * LANES), BF16),
            pltpu.VMEM((nc, HG_DIM, 2 * HG_DIM), F32),
            pltpu.VMEM((nc, SUBLANES, LANES), F32),
            pltpu.VMEM((nc, SUBLANES, LANES), F32),
            pltpu.VMEM((nc, HG_DIM, 2 * HG_DIM), BF16),
        ],
        compiler_params=_cparams("parallel", "parallel"),
        name="hgrn2",
    )(z3, z3, z3, k3, k3, gnorm3)


def _natten_kernel(q_ref, k_ref, v_ref, bias_ref, o_ref):
    rows = q_ref.shape[0] // GRID_W
    kh = min(NA_KH, rows)
    win = kh * GRID_W
    nt = (((1,), (1,)), ((), ()))
    lane = lax.broadcasted_iota(jnp.int32, (GRID_W, LANES), 1)
    first = lane < NA_DIM

    def group(gi, carry):
        scored = []
        for u in range(NA_GROUP):
            r = gi * NA_GROUP + u
            rs = jnp.clip(r - kh // 2, 0, rows - kh)
            q2 = q_ref[pl.ds(pl.multiple_of(r * GRID_W, GRID_W), GRID_W), :] * (NA_DIM ** -0.5)
            zero = jnp.zeros_like(q2)
            q_st = jnp.concatenate([jnp.where(first, q2, zero), jnp.where(first, zero, q2)], axis=0)
            kw = k_ref[pl.ds(pl.multiple_of(rs * GRID_W, GRID_W), win), :]
            scored.append((lax.dot_general(q_st, kw, nt, preferred_element_type=F32), r, rs))
        probs = []
        for s, r, rs in scored:
            bm = jnp.concatenate([bias_ref[0, r - rs], bias_ref[1, r - rs]], axis=0)
            s = jnp.where(bm > 0.5 * MASK_NEG, s + bm, MASK_NEG)
            m = jnp.max(s, axis=-1, keepdims=True)
            p = jnp.exp(s - m)
            probs.append((p.astype(BF16), jnp.sum(p, axis=-1, keepdims=True), r, rs))
        for p, l, r, rs in probs:
            vw = v_ref[pl.ds(pl.multiple_of(rs * GRID_W, GRID_W), win), :]
            o_st = jnp.dot(p, vw, preferred_element_type=F32) / l
            o = jnp.where(first, o_st[:GRID_W], o_st[GRID_W:])
            o_ref[pl.ds(pl.multiple_of(r * GRID_W, GRID_W), GRID_W), :] = o.astype(o_ref.dtype)
        return carry

    lax.fori_loop(0, rows // NA_GROUP, group, 0)


def _natten_bias_table(rpb, rows):
    kh = min(NA_KH, rows)
    r = np.arange(rows)
    rs = np.clip(r - kh // 2, 0, rows - kh)
    n_case = int((r - rs).max()) + 1
    case = np.arange(n_case)
    dr = np.arange(kh)[None, :] - case[:, None] + (NA_KH - 1)
    c = np.arange(GRID_W)
    col_start = np.clip(c - NA_KW // 2, 0, GRID_W - NA_KW)
    col_mask = (c[None, :] >= col_start[:, None]) & (c[None, :] < col_start[:, None] + NA_KW)
    dc = np.clip(c[None, :] - c[:, None], -(NA_KW - 1), NA_KW - 1) + (NA_KW - 1)
    depth, heads, _, n_dc = rpb.shape
    by_row = jnp.take(rpb.astype(F32).reshape(depth * heads, -1, n_dc), dr.reshape(-1), axis=1)
    by_row = by_row.reshape(depth * heads, n_case, kh, n_dc)
    pick = ((dc[None, :, :] == np.arange(n_dc)[:, None, None]) & col_mask[None]).astype(np.float32)
    masked = np.where(col_mask, 0.0, MASK_NEG).astype(np.float32)
    tab = jnp.einsum("xcjd,dqk->xcqjk", by_row, pick, precision=lax.Precision.HIGHEST)
    tab = tab + masked[None, None, :, None, :]
    return tab.reshape(depth, heads, n_case, GRID_W, kh * GRID_W)


def _natten(z3, bias_tab, layer, batch, seq):
    n_case, win = bias_tab.shape[2], bias_tab.shape[4]
    blk = lambda cb0: pl.BlockSpec((None, seq, LANES), lambda b, p: (cb0 + p, b, 0))
    return pl.pallas_call(
        _natten_kernel,
        grid=(batch, NA_HEADS // 2),
        in_specs=[
            blk(CB_NQ), blk(CB_NK), blk(CB_NV),
            pl.BlockSpec((None, 2, n_case, GRID_W, win), lambda b, p: (layer, p, 0, 0, 0)),
        ],
        out_specs=pl.BlockSpec((None, seq, LANES), lambda b, p: (p, b, 0)),
        out_shape=jax.ShapeDtypeStruct((NA_HEADS // 2, batch * seq, LANES), BF16),
        compiler_params=_cparams("parallel", "parallel"),
        name="natten",
    )(z3, z3, z3, bias_tab)


def _mem_attention(q_ref, kv_ref):
    nt = (((1,), (1,)), ((), ()))
    outs = []
    for h in range(CA_HEADS):
        s = lax.dot_general(q_ref[h], kv_ref[h], nt, preferred_element_type=F32) * (CA_DIM ** -0.5)
        m = jnp.max(s, axis=-1, keepdims=True)
        p = jnp.exp(s - m)
        l = jnp.sum(p, axis=-1, keepdims=True)
        o = jnp.dot(p.astype(BF16), kv_ref[CA_HEADS + h], preferred_element_type=F32) / l
        outs.append(o.astype(BF16))
    return jnp.concatenate(outs, axis=-1)


def _cat(ref):
    return jnp.concatenate([ref[j] for j in range(ref.shape[0])], axis=-1)


def _merge_kernel(x_ref, ohg_ref, ona_ref, cq_ref, kv_ref, ghg_ref, gna_ref, gca_ref,
                  whg_ref, wna_ref, wca_ref, wout_ref, o_ref):
    def branch(o, w_ref, g_ref):
        y = jnp.dot(o, w_ref[...], preferred_element_type=F32)
        return _sigmoid(_cat(g_ref).astype(F32)) * y

    merged = branch(_cat(ohg_ref), whg_ref, ghg_ref)
    merged = merged + branch(_cat(ona_ref), wna_ref, gna_ref)
    merged = merged + branch(_mem_attention(cq_ref, kv_ref), wca_ref, gca_ref)
    o_ref[...] = x_ref[...] + jnp.dot(merged.astype(BF16), wout_ref[...],
                                      preferred_element_type=F32)


def _merge(x2, ohg3, ona3, kv4, z3, w_hg_o, w_na_o, w_ca_o, w_out, layer, seq, tm):
    m, d = x2.shape
    ncb = d // LANES
    tiles_per_seq = seq // tm
    mem_len = kv4.shape[2] // (m // seq)
    act = lambda n: pl.BlockSpec((n, tm, LANES), lambda i: (0, i, 0))
    gate = lambda cb0: pl.BlockSpec((ncb, tm, LANES), lambda i: (cb0 // ncb, i, 0))
    full = lambda a: pl.BlockSpec((None,) + a.shape[1:], lambda i: (layer, 0, 0),
                                  pipeline_mode=pl.Buffered(1))
    return pl.pallas_call(
        _merge_kernel,
        grid=(m // tm,),
        in_specs=[
            pl.BlockSpec((tm, d), lambda i: (i, 0)),
            act(ohg3.shape[0]), act(ona3.shape[0]),
            pl.BlockSpec((CA_HEADS, tm, LANES), lambda i: (CB_CQ // CA_HEADS, i, 0)),
            pl.BlockSpec((None, 2 * CA_HEADS, mem_len, LANES),
                         lambda i: (layer, 0, i // tiles_per_seq, 0)),
            gate(CB_GHG), gate(CB_GNA), gate(CB_GCA),
            full(w_hg_o), full(w_na_o), full(w_ca_o), full(w_out),
        ],
        out_specs=pl.BlockSpec((tm, d), lambda i: (i, 0)),
        out_shape=jax.ShapeDtypeStruct((m, d), F32),
        compiler_params=_cparams("parallel"),
        name="merge",
    )(x2, ohg3, ona3, z3, kv4, z3, z3, z3, w_hg_o, w_na_o, w_ca_o, w_out)


def _ffn_kernel(x_ref, xp_ref, xn_ref, gain_ref, wup_ref, cw_ref, cb_ref, wd_ref, fin_ref, o_ref,
                h_ref, y_ref, *, tiles_per_seq, final_norm, ts):
    i = pl.program_id(0)
    tm = x_ref.shape[0]
    halo = SUBLANES

    def normed(x):
        ms = jnp.mean(x * x, axis=-1, keepdims=True)
        return x * lax.rsqrt(ms + EPS) * gain_ref[...]

    keep_prev = (i % tiles_per_seq != 0).astype(F32)
    keep_next = (i % tiles_per_seq != tiles_per_seq - 1).astype(F32)
    h_ref[0:halo, :] = (normed(xp_ref[...]) * keep_prev).astype(BF16)
    h_ref[halo:halo + tm, :] = normed(x_ref[...]).astype(BF16)
    h_ref[halo + tm:, :] = (normed(xn_ref[...]) * keep_next).astype(BF16)

    def conv_proj(cols):
        u = jnp.dot(h_ref[...], wup_ref[:, cols], preferred_element_type=F32)
        rows = u.shape[0]
        prev = pltpu.roll(u, 1, 0)[halo:halo + tm]
        nxt = pltpu.roll(u, rows - 1, 0)[halo:halo + tm]
        out = prev * cw_ref[0:1, cols]
        out = out + u[halo:halo + tm] * cw_ref[1:2, cols]
        out = out + nxt * cw_ref[2:3, cols]
        return out + cb_ref[:, cols]

    def sub_tile(s, carry):
        cols_a = pl.ds(pl.multiple_of(s * ts, ts), ts)
        cols_g = pl.ds(pl.multiple_of(D_FF + s * ts, ts), ts)
        a = conv_proj(cols_a)
        g = conv_proj(cols_g)
        c = np.float32(np.sqrt(2 / np.pi))
        th = jnp.tanh(a * ((a * a) * (c * np.float32(0.044715)) + c))
        half_ag = (a * g) * 0.5
        y_ref[:, cols_a] = (half_ag + half_ag * th).astype(BF16)
        return carry

    lax.fori_loop(0, D_FF // ts, sub_tile, 0, unroll=True)

    out = x_ref[...] + jnp.dot(y_ref[...], wd_ref[...], preferred_element_type=F32)
    if final_norm:
        ms = jnp.mean(out * out, axis=-1, keepdims=True)
        out = out * lax.rsqrt(ms + EPS) * fin_ref[...]
    o_ref[...] = out


def _conv_ffn(x2, gain3, w_up, conv_w, conv_b3, w_down, fin_gain, layer, seq, tm, ts, final_norm):
    m, d = x2.shape
    hb = tm // SUBLANES
    n_halo = m // SUBLANES
    kern = functools.partial(_ffn_kernel, tiles_per_seq=seq // tm, final_norm=final_norm, ts=ts)
    resident = lambda a: pl.BlockSpec((None,) + a.shape[1:], lambda i: (layer, 0, 0),
                                      pipeline_mode=pl.Buffered(1))
    return pl.pallas_call(
        kern,
        grid=(m // tm,),
        in_specs=[
            pl.BlockSpec((tm, d), lambda i: (i, 0)),
            pl.BlockSpec((SUBLANES, d), lambda i: (jnp.maximum(i * hb - 1, 0), 0)),
            pl.BlockSpec((SUBLANES, d), lambda i: (jnp.minimum((i + 1) * hb, n_halo - 1), 0)),
            pl.BlockSpec((None, 1, d), lambda i: (layer, 0, 0)),
            resident(w_up), resident(conv_w), resident(conv_b3), resident(w_down),
            pl.BlockSpec((1, d), lambda i: (0, 0)),
        ],
        out_specs=pl.BlockSpec((tm, d), lambda i: (i, 0)),
        out_shape=jax.ShapeDtypeStruct((m, d), F32),
        scratch_shapes=[
            pltpu.VMEM((tm + 2 * SUBLANES, d), BF16),
            pltpu.VMEM((tm, D_FF), BF16),
        ],
        compiler_params=_cparams("parallel"),
        name="conv_ffn",
    )(x2, x2, x2, gain3, w_up, conv_w, conv_b3, w_down, fin_gain.reshape(1, d))


def kernel(x, mem, norm_mix, w_in, hg_lb_logits, hg_gnorm, w_hg_o, na_rpb, w_na_o, mem_norm,
           w_mem_kv, w_ca_o, w_out, norm_ffn, w_up, conv_w, conv_b, w_down, norm_final):
    batch, seq, d = x.shape
    mem_len = mem.shape[1]
    depth = w_in.shape[0]
    assert d == D_MODEL and seq % GRID_W == 0 and seq % HG_CHUNK == 0
    hgw = HG_HEADS * HG_DIM

    p_lb = jax.nn.softmax(hg_lb_logits.astype(F32), axis=0)
    lower_bounds = jnp.clip(jnp.cumsum(p_lb, axis=0) - p_lb[0], 0.0, 1.0)

    bf = lambda a: a.astype(BF16)
    w_in, w_mem_kv, w_hg_o, w_na_o, w_ca_o, w_out, w_up, w_down = map(
        bf, (w_in, w_mem_kv, w_hg_o, w_na_o, w_ca_o, w_out, w_up, w_down))
    vec3 = lambda a: a.reshape(depth, 1, -1)
    norm_mix3, norm_ffn3, gnorm3, conv_b3 = map(vec3, (norm_mix, norm_ffn, hg_gnorm, conv_b))
    lb3 = vec3(lower_bounds)
    bias_tab = _natten_bias_table(na_rpb, seq // GRID_W)

    assert w_in.shape[-1] == (N_CB + 2 * hgw // LANES) * LANES

    x2 = x.reshape(batch * seq, d)
    kv4 = _mem_kv_proj(mem.reshape(batch * mem_len, d), mem_norm, w_mem_kv)

    for l in range(depth):
        z3, k3 = _in_proj(x2, norm_mix3, w_in, lb3, l, tm=ROW_TILE, tn=2 * hgw)

        ohg3 = _hgrn2(z3, k3, gnorm3, l, batch, seq)
        ona3 = _natten(z3, bias_tab, l, batch, seq)

        x2 = _merge(x2, ohg3, ona3, kv4, z3, w_hg_o, w_na_o, w_ca_o, w_out, l, seq, tm=ROW_TILE)
        x2 = _conv_ffn(x2, norm_ffn3, w_up, conv_w, conv_b3, w_down, norm_final, l, seq,
                       tm=ROW_TILE, ts=FFN_SUB, final_norm=(l == depth - 1))
    return x2.reshape(batch, seq, d)
```
